```python
import jax, jax.numpy as jnp
from jax import lax
import numpy as np

D_MODEL = 1024
BATCH = 4
SEQ = 8192
DEPTH = 1

MIX_WIDTH = D_MODEL
HEAD_DIM = 64
ATTN_WIDTH = MIX_WIDTH // 2
GMLP_WIDTH = MIX_WIDTH - ATTN_WIDTH
N_ATTN_HEADS = ATTN_WIDTH // HEAD_DIM
N_GMLP_HEADS = GMLP_WIDTH // HEAD_DIM
GMLP_HEAD_DIM = GMLP_WIDTH // N_GMLP_HEADS
IN_PROJ_WIDTH = 3 * ATTN_WIDTH + 2 * GMLP_WIDTH
MOBA_BLOCK = 256
MOBA_TOPK = 3
Q_CHUNK = 32
GMLP_CHUNK = 128
D_FF = 2816
ROPE_THETA = 10000.0
NORM_EPS = 1e-6
N_MOD = 9
NEG_INF = -1e30

kernel_name = "hymba_moba_gmlp_macaron_adaln"


def rmsnorm(x, g):
    xf = x.astype(jnp.float32)
    y = xf * lax.rsqrt(jnp.mean(xf * xf, axis=-1, keepdims=True) + NORM_EPS)
    return (y * g.astype(jnp.float32)).astype(x.dtype)


def layernorm(x, g, b):
    xf = x.astype(jnp.float32)
    mu = jnp.mean(xf, axis=-1, keepdims=True)
    var = jnp.mean(jnp.square(xf - mu), axis=-1, keepdims=True)
    y = (xf - mu) * lax.rsqrt(var + NORM_EPS)
    return (y * g.astype(jnp.float32) + b.astype(jnp.float32)).astype(x.dtype)


def modulate(x, shift, scale):
    return x * (1 + scale[:, None, :]) + shift[:, None, :]


def swiglu(x, w_gu, w_down):
    g, u = jnp.split(x @ w_gu, 2, axis=-1)
    return (jax.nn.silu(g) * u) @ w_down


def rope(x, pos):
    half = x.shape[-1] // 2
    inv_freq = ROPE_THETA ** (-jnp.arange(half, dtype=jnp.float32) / half)
    ang = pos[:, None] * inv_freq[None, :]
    cos = jnp.cos(ang)[None, :, None, :]
    sin = jnp.sin(ang)[None, :, None, :]
    xf = x.astype(jnp.float32)
    x1, x2 = xf[..., :half], xf[..., half:]
    return jnp.concatenate([x1 * cos - x2 * sin, x2 * cos + x1 * sin], axis=-1).astype(x.dtype)


def moba_attention(q, k, v):
    B, S, H, dh = q.shape
    n_blk = -(-S // MOBA_BLOCK)
    s_pad = n_blk * MOBA_BLOCK
    k_sel = min(MOBA_TOPK, n_blk)
    pad = ((0, 0), (0, s_pad - S), (0, 0), (0, 0))
    q, k, v = (jnp.pad(t, pad).transpose(0, 2, 1, 3) for t in (q, k, v))
    kb = k.reshape(B, H, n_blk, MOBA_BLOCK, dh)
    vb = v.reshape(B, H, n_blk, MOBA_BLOCK, dh)
    k_mean = jnp.mean(kb.astype(jnp.float32), axis=3)
    n_chunks = s_pad // Q_CHUNK
    q_chunks = q.reshape(B, H, n_chunks, Q_CHUNK, dh).transpose(2, 0, 1, 3, 4)
    scale = dh ** -0.5
    b_idx = jnp.arange(B)[:, None, None, None]
    h_idx = jnp.arange(H)[None, :, None, None]
    blk_ids = jnp.arange(n_blk)

    def one_chunk(args):
        qc, ci = args
        q_pos = ci * Q_CHUNK + jnp.arange(Q_CHUNK)
        own = q_pos[0] // MOBA_BLOCK
        gate = jnp.einsum('bhqd,bhnd->bhqn', qc.astype(jnp.float32), k_mean)
        gate = jnp.where(blk_ids < own, gate, NEG_INF)
        _, top_idx = lax.top_k(gate, k_sel)
        sel_valid = jnp.arange(k_sel) < own
        own_idx = jnp.broadcast_to(own, top_idx.shape[:-1] + (1,)).astype(top_idx.dtype)
        idx = jnp.concatenate([top_idx, own_idx], axis=-1)
        kg = kb[b_idx, h_idx, idx]
        vg = vb[b_idx, h_idx, idx]
        s = jnp.einsum('bhqd,bhqnkd->bhqnk', qc, kg).astype(jnp.float32) * scale
        key_pos = own * MOBA_BLOCK + jnp.arange(MOBA_BLOCK)
        own_mask = key_pos[None, :] <= q_pos[:, None]
        sel_mask = jnp.broadcast_to(sel_valid[None, :, None], (Q_CHUNK, k_sel, MOBA_BLOCK))
        mask = jnp.concatenate([sel_mask, own_mask[:, None, :]], axis=1)
        s = jnp.where(mask, s, NEG_INF)
        p = jax.nn.softmax(s.reshape(B, H, Q_CHUNK, -1), axis=-1).reshape(s.shape)
        return jnp.einsum('bhqnk,bhqnkd->bhqd', p.astype(vg.dtype), vg)

    out = lax.map(one_chunk, (q_chunks, jnp.arange(n_chunks)))
    out = out.transpose(1, 0, 3, 2, 4).reshape(B, s_pad, H, dh)[:, :S]
    return out.reshape(B, S, H * dh)


def gmlp_spatial_gating(u, v, ln_g, ln_b, w_s, b_s):
    B, S, _ = u.shape
    v = layernorm(v, ln_g, ln_b)
    n_c = S // GMLP_CHUNK
    causal = jnp.tril(jnp.ones((GMLP_CHUNK, GMLP_CHUNK), dtype=bool))
    w = jnp.where(causal[None], w_s, 0)
    vc = v.reshape(B, n_c, GMLP_CHUNK, N_GMLP_HEADS, GMLP_HEAD_DIM)
    mixed = jnp.einsum('hts,bnshd->bnthd', w, vc) + b_s.T[None, None, :, :, None]
    return u * mixed.reshape(B, S, GMLP_WIDTH)


def setup_inputs(seed: int = 0) -> dict:
    key = jax.random.key(seed)
    ks = jax.random.split(key, 20)
    f32 = jnp.float32
    nrm = lambda k, shape, s: jax.random.normal(k, shape, f32) * s
    L, D = DEPTH, D_MODEL
    return {
        "x": nrm(ks[0], (BATCH, SEQ, D), 1.0),
        "c": nrm(ks[1], (BATCH, D), 1.0),
        "w_ada": nrm(ks[2], (L, D, N_MOD * D), 0.5 * D ** -0.5),
        "b_ada": nrm(ks[3], (L, N_MOD * D), 0.01),
        "norm_ffn1": 1.0 + nrm(ks[4], (L, D), 0.01),
        "w_ffn1_gu": nrm(ks[5], (L, D, 2 * D_FF), D ** -0.5),
        "w_ffn1_down": nrm(ks[6], (L, D_FF, D), D_FF ** -0.5),
        "norm_mix": 1.0 + nrm(ks[7], (L, D), 0.01),
        "w_in": nrm(ks[8], (L, D, IN_PROJ_WIDTH), D ** -0.5),
        "gmlp_ln_g": 1.0 + nrm(ks[9], (L, GMLP_WIDTH), 0.01),
        "gmlp_ln_b": nrm(ks[10], (L, GMLP_WIDTH), 0.01),
        "gmlp_w_s": nrm(ks[11], (L, N_GMLP_HEADS, GMLP_CHUNK, GMLP_CHUNK), GMLP_CHUNK ** -0.5),
        "gmlp_b_s": 1.0 + nrm(ks[12], (L, N_GMLP_HEADS, GMLP_CHUNK), 0.01),
        "g_attn_out": 1.0 + nrm(ks[13], (L, ATTN_WIDTH), 0.01),
        "g_gmlp_out": 1.0 + nrm(ks[14], (L, GMLP_WIDTH), 0.01),
        "w_out": nrm(ks[15], (L, MIX_WIDTH, D), MIX_WIDTH ** -0.5),
        "norm_ffn2": 1.0 + nrm(ks[16], (L, D), 0.01),
        "w_ffn2_gu": nrm(ks[17], (L, D, 2 * D_FF), D ** -0.5),
        "w_ffn2_down": nrm(ks[18], (L, D_FF, D), D_FF ** -0.5),
        "norm_final": 1.0 + nrm(ks[19], (D,), 0.01),
    }


def reference(x, c, w_ada, b_ada, norm_ffn1, w_ffn1_gu, w_ffn1_down, norm_mix, w_in,
              gmlp_ln_g, gmlp_ln_b, gmlp_w_s, gmlp_b_s, g_attn_out, g_gmlp_out, w_out,
              norm_ffn2, w_ffn2_gu, w_ffn2_down, norm_final):
    B, S, D = x.shape
    pos = jnp.arange(S, dtype=jnp.float32)
    c_act = jax.nn.silu(c)
    h = x
    for l in range(DEPTH):
        mod = c_act @ w_ada[l] + b_ada[l]
        (sh1, sc1, gt1, sh2, sc2, gt2, sh3, sc3, gt3) = jnp.split(mod, N_MOD, axis=-1)

        y = modulate(rmsnorm(h, norm_ffn1[l]), sh1, sc1)
        h = h + 0.5 * gt1[:, None, :] * swiglu(y, w_ffn1_gu[l], w_ffn1_down[l])

        y = modulate(rmsnorm(h, norm_mix[l]), sh2, sc2)
        proj = y @ w_in[l]
        q, k, v, gu, gv = jnp.split(
            proj, np.cumsum([ATTN_WIDTH, ATTN_WIDTH, ATTN_WIDTH, GMLP_WIDTH]).tolist(), axis=-1)
        q = rope(q.reshape(B, S, N_ATTN_HEADS, HEAD_DIM), pos)
        k = rope(k.reshape(B, S, N_ATTN_HEADS, HEAD_DIM), pos)
        v = v.reshape(B, S, N_ATTN_HEADS, HEAD_DIM)
        attn_out = moba_attention(q, k, v)
        gmlp_out = gmlp_spatial_gating(jax.nn.gelu(gu), jax.nn.gelu(gv), gmlp_ln_g[l],
                                       gmlp_ln_b[l], gmlp_w_s[l], gmlp_b_s[l])
        merged = jnp.concatenate([rmsnorm(attn_out, g_attn_out[l]),
                                  rmsnorm(gmlp_out, g_gmlp_out[l])], axis=-1)
        h = h + gt2[:, None, :] * (merged @ w_out[l])

        y = modulate(rmsnorm(h, norm_ffn2[l]), sh3, sc3)
        h = h + 0.5 * gt3[:, None, :] * swiglu(y, w_ffn2_gu[l], w_ffn2_down[l])
    return rmsnorm(h, norm_final)
```

```python
import functools

import jax
import jax.numpy as jnp
import numpy as np
from jax import lax
from jax.experimental import pallas as pl
from jax.experimental.pallas import tpu as pltpu

HEAD_DIM = 64
MOBA_BLOCK = 256
MOBA_TOPK = 3
ROPE_THETA = 10000.0
NORM_EPS = 1e-6
N_MOD = 9
NEG_INF = -1e30
LANES = 128
VMEM_LIMIT_BYTES = 56 * 2**20

F32 = jnp.float32
BF16 = jnp.bfloat16


def _cparams(n_axes):
    return pltpu.CompilerParams(dimension_semantics=("arbitrary",) * n_axes,
                                vmem_limit_bytes=VMEM_LIMIT_BYTES)


def _resident(shape):
    nd = len(shape)
    return pl.BlockSpec(shape, lambda *_: (0,) * nd, pipeline_mode=pl.Buffered(1))


def _rmsnorm(x, g):
    return x * lax.rsqrt(jnp.mean(x * x, axis=-1, keepdims=True) + NORM_EPS) * g


def _ada_kernel(c_ref, w_ref, b_ref, o_ref):
    ca = jax.nn.silu(c_ref[...])
    o_ref[...] = jnp.dot(ca, w_ref[...], precision=lax.Precision.HIGHEST,
                         preferred_element_type=F32) + b_ref[...]


def _ada(c, w_ada, b_ada):
    B, D = c.shape
    n_out = w_ada.shape[1]
    rows = -(-B // 8) * 8
    c_pad = jnp.pad(c, ((0, rows - B), (0, 0)))
    tn = n_out // 8 if (n_out // 8) % LANES == 0 else n_out
    mod = pl.pallas_call(
        _ada_kernel,
        grid=(n_out // tn,),
        in_specs=[pl.BlockSpec((rows, D), lambda j: (0, 0)),
                  pl.BlockSpec((D, tn), lambda j: (0, j)),
                  pl.BlockSpec((1, tn), lambda j: (0, j))],
        out_specs=pl.BlockSpec((rows, tn), lambda j: (0, j)),
        out_shape=jax.ShapeDtypeStruct((rows, n_out), F32),
        compiler_params=_cparams(1),
        name="ada",
    )(c_pad, w_ada, b_ada.reshape(1, n_out))
    return mod[:B].reshape(B, N_MOD, D)


def _ffn_kernel(h_ref, mod_ref, nw_ref, wgu_ref, wd_ref, nf_ref, o_ref, *, mod_base, d_ff, tf, final_norm):
    x = h_ref[...]
    sh = mod_ref[0, mod_base:mod_base + 1, :]
    sc = mod_ref[0, mod_base + 1:mod_base + 2, :]
    gt = mod_ref[0, mod_base + 2:mod_base + 3, :]
    y = (_rmsnorm(x, nw_ref[...]) * (1.0 + sc) + sh).astype(BF16)
    acc = jnp.zeros(x.shape, F32)
    for c in range(d_ff // tf):
        g = jnp.dot(y, wgu_ref[:, c * tf:(c + 1) * tf], preferred_element_type=F32)
        u = jnp.dot(y, wgu_ref[:, d_ff + c * tf:d_ff + (c + 1) * tf], preferred_element_type=F32)
        mid = (jax.nn.silu(g) * u).astype(BF16)
        acc = acc + jnp.dot(mid, wd_ref[c * tf:(c + 1) * tf, :], preferred_element_type=F32)
    out = x + (0.5 * gt) * acc
    if final_norm:
        out = _rmsnorm(out, nf_ref[...])
    o_ref[...] = out


def _ffn(h, mod, norm_w, w_gu, w_down, norm_final, *, seq, mod_base, final_norm, tm):
    n_tok, D = h.shape
    d_ff = w_down.shape[0]
    tf = 256 if d_ff % 256 == 0 else d_ff
    kern = functools.partial(_ffn_kernel, mod_base=mod_base, d_ff=d_ff, tf=tf, final_norm=final_norm)
    return pl.pallas_call(
        kern,
        grid=(n_tok // tm,),
        in_specs=[pl.BlockSpec((tm, D), lambda i: (i, 0)),
                  pl.BlockSpec((1, N_MOD, D), lambda i: (i * tm // seq, 0, 0)),
                  _resident((1, D)),
                  _resident((D, 2 * d_ff)),
                  _resident((d_ff, D)),
                  _resident((1, D))],
        out_specs=pl.BlockSpec((tm, D), lambda i: (i, 0)),
        out_shape=jax.ShapeDtypeStruct((n_tok, D), F32),
        compiler_params=_cparams(1),
        name="ffn_final" if final_norm else "ffn",
    )(h, mod, norm_w.reshape(1, D), w_gu.astype(BF16), w_down.astype(BF16), norm_final.reshape(1, D))


def _rope(x, cos, sin_signed, first_half):
    partner = jnp.where(first_half, pltpu.roll(x, LANES - HEAD_DIM // 2, 1), pltpu.roll(x, HEAD_DIM // 2, 1))
    return x * cos + partner * sin_signed


def _inproj_kernel(h_ref, mod_ref, nw_ref, win_ref, cos_ref, sin_ref, lng_ref, lnb_ref,
                   qt_ref, k_ref, vt_ref, u_ref, vln_ref, *, aw, gw):
    x = h_ref[...]
    sh = mod_ref[0, 3:4, :]
    sc = mod_ref[0, 4:5, :]
    y = (_rmsnorm(x, nw_ref[...]) * (1.0 + sc) + sh).astype(BF16)
    proj = jnp.dot(y, win_ref[...], preferred_element_type=F32)
    cos = cos_ref[...]
    sin_signed = sin_ref[...]
    lane = lax.broadcasted_iota(jnp.int32, cos.shape, 1)
    first_half = (lane % HEAD_DIM) < (HEAD_DIM // 2)
    scale = HEAD_DIM ** -0.5
    tm = x.shape[0]
    for s in range(aw // LANES):
        qs = _rope(proj[:, s * LANES:(s + 1) * LANES], cos, sin_signed, first_half) * scale
        ks = _rope(proj[:, aw + s * LANES:aw + (s + 1) * LANES], cos, sin_signed, first_half)
        k_ref[2 * s] = ks[:, :HEAD_DIM].astype(BF16)
        k_ref[2 * s + 1] = ks[:, HEAD_DIM:].astype(BF16)
        vs = proj[:, 2 * aw + s * LANES:2 * aw + (s + 1) * LANES]
        for sb in range(tm // MOBA_BLOCK):
            rows = slice(sb * MOBA_BLOCK, (sb + 1) * MOBA_BLOCK)
            qt_ref[sb, s * LANES:(s + 1) * LANES, :] = qs[rows].T.astype(BF16)
            vt_ref[sb, s * LANES:(s + 1) * LANES, :] = vs[rows].T.astype(BF16)
    u_ref[...] = jax.nn.gelu(proj[:, 3 * aw:3 * aw + gw])
    gv = jax.nn.gelu(proj[:, 3 * aw + gw:3 * aw + 2 * gw])
    mu = jnp.mean(gv, axis=-1, keepdims=True)
    var = jnp.mean(jnp.square(gv - mu), axis=-1, keepdims=True)
    vln = (gv - mu) * lax.rsqrt(var + NORM_EPS) * lng_ref[...] + lnb_ref[...]
    vln_ref[...] = vln.astype(BF16)


def _rope_tables(seq):
    half = HEAD_DIM // 2
    pos = jnp.arange(seq, dtype=F32)
    inv_freq = ROPE_THETA ** (-jnp.arange(half, dtype=F32) / half)
    ang = pos[:, None] * inv_freq[None, :]
    cos, sin = jnp.cos(ang), jnp.sin(ang)
    reps = LANES // HEAD_DIM
    return (jnp.tile(jnp.concatenate([cos, cos], axis=-1), (1, reps)),
            jnp.tile(jnp.concatenate([-sin, sin], axis=-1), (1, reps)))


def _inproj(h, mod, norm_w, w_in, ln_g, ln_b, *, batch, seq, aw, gw, tm):
    n_tok, D = h.shape
    n_heads = aw // HEAD_DIM
    st = seq // tm
    bpt = tm // MOBA_BLOCK
    n_blk = seq // MOBA_BLOCK
    cos_t, sin_t = _rope_tables(seq)
    kern = functools.partial(_inproj_kernel, aw=aw, gw=gw)
    return pl.pallas_call(
        kern,
        grid=(n_tok // tm,),
        in_specs=[pl.BlockSpec((tm, D), lambda i: (i, 0)),
                  pl.BlockSpec((1, N_MOD, D), lambda i: (i // st, 0, 0)),
                  _resident((1, D)),
                  _resident((D, 3 * aw + 2 * gw)),
                  pl.BlockSpec((tm, LANES), lambda i: (i % st, 0)),
                  pl.BlockSpec((tm, LANES), lambda i: (i % st, 0)),
                  _resident((1, gw)),
                  _resident((1, gw))],
        out_specs=[pl.BlockSpec((None, bpt, aw, MOBA_BLOCK), lambda i: (i // st, i % st, 0, 0)),
                   pl.BlockSpec((None, n_heads, tm, HEAD_DIM), lambda i: (i // st, 0, i % st, 0)),
                   pl.BlockSpec((None, bpt, aw, MOBA_BLOCK), lambda i: (i // st, i % st, 0, 0)),
                   pl.BlockSpec((tm, gw), lambda i: (i, 0)),
                   pl.BlockSpec((tm, gw), lambda i: (i, 0))],
        out_shape=[jax.ShapeDtypeStruct((batch, n_blk, aw, MOBA_BLOCK), BF16),
                   jax.ShapeDtypeStruct((batch, n_heads, seq, HEAD_DIM), BF16),
                   jax.ShapeDtypeStruct((batch, n_blk, aw, MOBA_BLOCK), BF16),
                   jax.ShapeDtypeStruct((n_tok, gw), F32),
                   jax.ShapeDtypeStruct((n_tok, gw), BF16)],
        compiler_params=_cparams(1),
        name="inproj",
    )(h, mod, norm_w.reshape(1, D), w_in.astype(BF16), cos_t, sin_t, ln_g.reshape(1, gw), ln_b.reshape(1, gw))


HEADS_PER_STEP = LANES // HEAD_DIM


def _moba_kernel(qt_ref, k_ref, vt_ref, o_ref, km_ref, sel_ref, *, n_blk):
    i = pl.program_id(2)
    blk = MOBA_BLOCK

    @pl.when(i == 0)
    def _():
        for hh in range(HEADS_PER_STEP):
            for j in range(n_blk):
                kj = k_ref[hh, j * blk:(j + 1) * blk, :].astype(F32)
                km_ref[hh, j:j + 1, :] = jnp.mean(kj, axis=0, keepdims=True)

    qts = [qt_ref[hh * HEAD_DIM:(hh + 1) * HEAD_DIM, :] for hh in range(HEADS_PER_STEP)]

    blk_id = lax.broadcasted_iota(jnp.int32, (n_blk, blk), 0)
    for hh in range(HEADS_PER_STEP):
        gate = jnp.dot(km_ref[hh], qts[hh].astype(F32), precision=lax.Precision.HIGHEST,
                       preferred_element_type=F32)
        gate = jnp.where(blk_id < i, gate, NEG_INF)
        sel = jnp.zeros((n_blk, blk), F32)
        for r in range(min(MOBA_TOPK, n_blk)):
            top = jnp.max(gate, axis=0, keepdims=True)
            idx = jnp.min(jnp.where(gate == top, blk_id, n_blk), axis=0, keepdims=True)
            hit = blk_id == idx
            sel = jnp.where(hit, jnp.where(r < i, 1.0, 0.0), sel)
            gate = jnp.where(hit, -jnp.inf, gate)
        sel_ref[hh] = sel

    def attend(kj, vtj, qt, mask, m, l, acc):
        s = jnp.dot(kj, qt, preferred_element_type=F32)
        s = jnp.where(mask, s, NEG_INF)
        m_new = jnp.maximum(m, jnp.max(s, axis=0, keepdims=True))
        alpha = jnp.exp(m - m_new)
        p = jnp.exp(s - m_new)
        l = alpha * l + jnp.sum(p, axis=0, keepdims=True)
        acc = alpha * acc + jnp.dot(vtj, p.astype(BF16), preferred_element_type=F32)
        return m_new, l, acc

    key_pos = lax.broadcasted_iota(jnp.int32, (blk, blk), 0)
    q_pos = lax.broadcasted_iota(jnp.int32, (blk, blk), 1)
    causal = key_pos <= q_pos
    own = pl.multiple_of(i * blk, blk)
    state = []
    for hh in range(HEADS_PER_STEP):
        m0 = jnp.full((1, blk), NEG_INF, F32)
        l0 = jnp.zeros((1, blk), F32)
        a0 = jnp.zeros((HEAD_DIM, blk), F32)
        state += list(attend(k_ref[hh, pl.ds(own, blk), :],
                             vt_ref[i, hh * HEAD_DIM:(hh + 1) * HEAD_DIM, :],
                             qts[hh], causal, m0, l0, a0))

    def body(j, carry):
        start = pl.multiple_of(j * blk, blk)
        out = []
        for hh in range(HEADS_PER_STEP):
            m, l, acc = carry[3 * hh:3 * hh + 3]
            mask = sel_ref[hh, pl.ds(j, 1), :] > 0.5
            out += list(attend(k_ref[hh, pl.ds(start, blk), :],
                               vt_ref[j, hh * HEAD_DIM:(hh + 1) * HEAD_DIM, :],
                               qts[hh], mask, m, l, acc))
        return tuple(out)

    state = lax.fori_loop(0, i, body, tuple(state))
    outs = [state[3 * hh + 2] * (1.0 / state[3 * hh + 1]) for hh in range(HEADS_PER_STEP)]
    o_ref[...] = jnp.concatenate(outs, axis=0).T


def _moba(qt, k, vt, *, batch, seq, aw):
    n_blk = seq // MOBA_BLOCK
    n_hp = aw // LANES
    kern = functools.partial(_moba_kernel, n_blk=n_blk)
    return pl.pallas_call(
        kern,
        grid=(batch, n_hp, n_blk),
        in_specs=[pl.BlockSpec((None, None, LANES, MOBA_BLOCK), lambda b, hp, i: (b, i, hp, 0)),
                  pl.BlockSpec((None, HEADS_PER_STEP, seq, HEAD_DIM), lambda b, hp, i: (b, hp, 0, 0)),
                  pl.BlockSpec((None, n_blk, LANES, MOBA_BLOCK), lambda b, hp, i: (b, 0, hp, 0))],
        out_specs=pl.BlockSpec((None, MOBA_BLOCK, LANES), lambda b, hp, i: (b, i, hp)),
        out_shape=jax.ShapeDtypeStruct((batch, seq, aw), F32),
        scratch_shapes=[pltpu.VMEM((HEADS_PER_STEP, n_blk, HEAD_DIM), F32),
                        pltpu.VMEM((HEADS_PER_STEP, n_blk, MOBA_BLOCK), F32)],
        compiler_params=_cparams(3),
        name="moba",
    )(qt, k, vt)


def _mixout_kernel(h_ref, mod_ref, a_ref, u_ref, vln_ref, ws_ref, bias_ref, ga_ref, gg_ref, wout_ref, o_ref,
                   *, aw, gw, chunk):
    tm = h_ref.shape[0]
    gt = mod_ref[0, 5:6, :]
    attn_n = _rmsnorm(a_ref[...], ga_ref[...]).astype(BF16)

    n_grp = gw // LANES
    row = lax.broadcasted_iota(jnp.int32, (HEADS_PER_STEP * chunk, chunk), 0)
    col = lax.broadcasted_iota(jnp.int32, (HEADS_PER_STEP * chunk, chunk), 1)
    keep = (row % chunk) >= col
    w_pairs = [jnp.where(keep, ws_ref[g], 0.0).astype(BF16) for g in range(n_grp)]
    lane = lax.broadcasted_iota(jnp.int32, (chunk, LANES), 1)
    first_head = lane < HEAD_DIM
    bias = bias_ref[...]

    rows = []
    for c in range(tm // chunk):
        vc = vln_ref[c * chunk:(c + 1) * chunk, :]
        mixed = []
        for g in range(n_grp):
            r = jnp.dot(w_pairs[g], vc[:, g * LANES:(g + 1) * LANES], preferred_element_type=F32)
            mixed.append(jnp.where(first_head, r[:chunk], r[chunk:]))
        mixed = jnp.concatenate(mixed, axis=1) + bias
        rows.append(u_ref[c * chunk:(c + 1) * chunk, :] * mixed)
    gm = jnp.concatenate(rows, axis=0)
    gmlp_n = _rmsnorm(gm, gg_ref[...]).astype(BF16)

    y = jnp.dot(attn_n, wout_ref[:aw, :], preferred_element_type=F32)
    y = y + jnp.dot(gmlp_n, wout_ref[aw:, :], preferred_element_type=F32)
    o_ref[...] = h_ref[...] + gt * y


def _mixout(h, mod, attn, u, vln, w_s, b_s, g_attn, g_gmlp, w_out, *, seq, tm):
    n_tok, D = h.shape
    aw, gw = g_attn.shape[0], g_gmlp.shape[0]
    n_gh, chunk = w_s.shape[0], w_s.shape[1]
    ws_pairs = w_s.reshape(n_gh // HEADS_PER_STEP, HEADS_PER_STEP * chunk, chunk)
    bias = jnp.repeat(b_s.T, gw // n_gh, axis=1)
    kern = functools.partial(_mixout_kernel, aw=aw, gw=gw, chunk=chunk)
    return pl.pallas_call(
        kern,
        grid=(n_tok // tm,),
        in_specs=[pl.BlockSpec((tm, D), lambda i: (i, 0)),
                  pl.BlockSpec((1, N_MOD, D), lambda i: (i * tm // seq, 0, 0)),
                  pl.BlockSpec((tm, aw), lambda i: (i, 0)),
                  pl.BlockSpec((tm, gw), lambda i: (i, 0)),
                  pl.BlockSpec((tm, gw), lambda i: (i, 0)),
                  _resident(ws_pairs.shape),
                  _resident((chunk, gw)),
                  _resident((1, aw)),
                  _resident((1, gw)),
                  _resident((aw + gw, D))],
        out_specs=pl.BlockSpec((tm, D), lambda i: (i, 0)),
        out_shape=jax.ShapeDtypeStruct((n_tok, D), F32),
        compiler_params=_cparams(1),
        name="mixout",
    )(h, mod, attn, u, vln, ws_pairs, bias, g_attn.reshape(1, aw), g_gmlp.reshape(1, gw), w_out.astype(BF16))


def kernel(x, c, w_ada, b_ada, norm_ffn1, w_ffn1_gu, w_ffn1_down, norm_mix, w_in, gmlp_ln_g, gmlp_ln_b,
           gmlp_w_s, gmlp_b_s, g_attn_out, g_gmlp_out, w_out, norm_ffn2, w_ffn2_gu, w_ffn2_down, norm_final):
    B, S, D = x.shape
    depth = w_ada.shape[0]
    aw, gw = g_attn_out.shape[1], g_gmlp_out.shape[1]
    assert S % MOBA_BLOCK == 0 and aw % LANES == 0 and gw % LANES == 0
    tm = 512 if S % 512 == 0 else MOBA_BLOCK
    h = x.reshape(B * S, D)
    for l in range(depth):
        last = l == depth - 1
        mod = _ada(c, w_ada[l], b_ada[l])
        h = _ffn(h, mod, norm_ffn1[l], w_ffn1_gu[l], w_ffn1_down[l], norm_final,
                 seq=S, mod_base=0, final_norm=False, tm=tm)
        qt, k, vt, u, vln = _inproj(h, mod, norm_mix[l], w_in[l], gmlp_ln_g[l], gmlp_ln_b[l],
                                    batch=B, seq=S, aw=aw, gw=gw, tm=tm)
        attn = _moba(qt, k, vt, batch=B, seq=S, aw=aw).reshape(B * S, aw)
        h = _mixout(h, mod, attn, u, vln, gmlp_w_s[l], gmlp_b_s[l], g_attn_out[l], g_gmlp_out[l], w_out[l],
                    seq=S, tm=tm)
        h = _ffn(h, mod, norm_ffn2[l], w_ffn2_gu[l], w_ffn2_down[l], norm_final,
                 seq=S, mod_base=6, final_norm=last, tm=tm)
    return h.reshape(B, S, D)
```

```python
import functools

import jax
import jax.numpy as jnp
import numpy as np
from jax import lax
from jax.experimental import pallas as pl
from jax.experimental.pallas import tpu as pltpu

HEAD_DIM = 64
MOBA_BLOCK = 256
MOBA_TOPK = 3
ROPE_THETA = 10000.0
NORM_EPS = 1e-6
N_MOD = 9
NEG_INF = -1e30
LANES = 128
HEADS_PER_STEP = LANES // HEAD_DIM
KV_GROUP = 4
VMEM_LIMIT_BYTES = 56 * 2**20

F32 = jnp.float32
BF16 = jnp.bfloat16


def _cparams(n_axes):
    return pltpu.CompilerParams(dimension_semantics=("arbitrary",) * n_axes,
                                vmem_limit_bytes=VMEM_LIMIT_BYTES)


def _resident(shape):
    nd = len(shape)
    return pl.BlockSpec(shape, lambda *_: (0,) * nd, pipeline_mode=pl.Buffered(1))


def _rmsnorm(x, g):
    return x * lax.rsqrt(jnp.mean(x * x, axis=-1, keepdims=True) + NORM_EPS) * g


def _ada_kernel(c_ref, w_ref, b_ref, o_ref):
    ca = jax.nn.silu(c_ref[...])
    o_ref[...] = jnp.dot(ca, w_ref[...], precision=lax.Precision.HIGHEST,
                         preferred_element_type=F32) + b_ref[...]


def _ada(c, w_ada, b_ada):
    B, D = c.shape
    n_out = w_ada.shape[1]
    rows = -(-B // 8) * 8
    c_pad = jnp.pad(c, ((0, rows - B), (0, 0)))
    tn = n_out // 8 if (n_out // 8) % LANES == 0 else n_out
    mod = pl.pallas_call(
        _ada_kernel,
        grid=(n_out // tn,),
        in_specs=[pl.BlockSpec((rows, D), lambda j: (0, 0)),
                  pl.BlockSpec((D, tn), lambda j: (0, j)),
                  pl.BlockSpec((1, tn), lambda j: (0, j))],
        out_specs=pl.BlockSpec((rows, tn), lambda j: (0, j)),
        out_shape=jax.ShapeDtypeStruct((rows, n_out), F32),
        compiler_params=_cparams(1),
        name="ada",
    )(c_pad, w_ada, b_ada.reshape(1, n_out))
    return mod[:B].reshape(B, N_MOD, D)


def _ffn_kernel(h_ref, mod_ref, nw_ref, wgu_ref, wd_ref, nf_ref, o_ref, *, mod_base, d_ff, tf, final_norm):
    x = h_ref[...]
    sh = mod_ref[0, mod_base:mod_base + 1, :]
    sc = mod_ref[0, mod_base + 1:mod_base + 2, :]
    gt = mod_ref[0, mod_base + 2:mod_base + 3, :]
    y = (_rmsnorm(x, nw_ref[...]) * (1.0 + sc) + sh).astype(BF16)
    acc = jnp.zeros(x.shape, F32)
    for c in range(d_ff // tf):
        g = jnp.dot(y, wgu_ref[:, c * tf:(c + 1) * tf], preferred_element_type=F32)
        u = jnp.dot(y, wgu_ref[:, d_ff + c * tf:d_ff + (c + 1) * tf], preferred_element_type=F32)
        mid = (jax.nn.silu(g) * u).astype(BF16)
        acc = acc + jnp.dot(mid, wd_ref[c * tf:(c + 1) * tf, :], preferred_element_type=F32)
    out = x + (0.5 * gt) * acc
    if final_norm:
        out = _rmsnorm(out, nf_ref[...])
    o_ref[...] = out


def _ffn(h, mod, norm_w, w_gu, w_down, norm_final, *, seq, mod_base, final_norm, tm):
    n_tok, D = h.shape
    d_ff = w_down.shape[0]
    tf = 256 if d_ff % 256 == 0 else d_ff
    kern = functools.partial(_ffn_kernel, mod_base=mod_base, d_ff=d_ff, tf=tf, final_norm=final_norm)
    return pl.pallas_call(
        kern,
        grid=(n_tok // tm,),
        in_specs=[pl.BlockSpec((tm, D), lambda i: (i, 0)),
                  pl.BlockSpec((1, N_MOD, D), lambda i: (i * tm // seq, 0, 0)),
                  _resident((1, D)),
                  _resident((D, 2 * d_ff)),
                  _resident((d_ff, D)),
                  _resident((1, D))],
        out_specs=pl.BlockSpec((tm, D), lambda i: (i, 0)),
        out_shape=jax.ShapeDtypeStruct((n_tok, D), F32),
        compiler_params=_cparams(1),
        name="ffn_final" if final_norm else "ffn",
    )(h, mod, norm_w.reshape(1, D), w_gu.astype(BF16), w_down.astype(BF16), norm_final.reshape(1, D))


def _rope(x, cos, sin_signed, first_half):
    partner = jnp.where(first_half, pltpu.roll(x, LANES - HEAD_DIM // 2, 1), pltpu.roll(x, HEAD_DIM // 2, 1))
    return x * cos + partner * sin_signed


def _inproj_kernel(h_ref, mod_ref, nw_ref, win_ref, cos_ref, sin_ref, lng_ref, lnb_ref,
                   qt_ref, k_ref, vt_ref, u_ref, vln_ref, *, aw, gw, st):
    x = h_ref[...]
    sh = mod_ref[0, 3:4, :]
    sc = mod_ref[0, 4:5, :]
    y = (_rmsnorm(x, nw_ref[...]) * (1.0 + sc) + sh).astype(BF16)
    proj = jnp.dot(y, win_ref[...], preferred_element_type=F32)
    cos = cos_ref[...]
    sin_signed = sin_ref[...]
    tm = x.shape[0]
    lane = lax.broadcasted_iota(jnp.int32, cos.shape, 1)
    first_half = (lane % HEAD_DIM) < (HEAD_DIM // 2)
    is_key_lane = lane < HEAD_DIM
    row = lax.broadcasted_iota(jnp.int32, cos.shape, 0)
    blk_of_row = (pl.program_id(0) % st) * (tm // MOBA_BLOCK) + row // MOBA_BLOCK
    blk_onehot = jnp.where(lane - HEAD_DIM == blk_of_row, 1.0, 0.0)
    scale = HEAD_DIM ** -0.5 * np.log2(np.e)
    for s in range(aw // LANES):
        qs = _rope(proj[:, s * LANES:(s + 1) * LANES], cos, sin_signed, first_half) * scale
        ks = _rope(proj[:, aw + s * LANES:aw + (s + 1) * LANES], cos, sin_signed, first_half)
        k_ref[2 * s] = jnp.where(is_key_lane, ks, blk_onehot).astype(BF16)
        k_ref[2 * s + 1] = jnp.where(is_key_lane, pltpu.roll(ks, HEAD_DIM, 1), blk_onehot).astype(BF16)
        vs = proj[:, 2 * aw + s * LANES:2 * aw + (s + 1) * LANES]
        vt_ref[s * LANES:(s + 1) * LANES, :] = vs.T.astype(BF16)
        for sb in range(tm // MOBA_BLOCK):
            rows = slice(sb * MOBA_BLOCK, (sb + 1) * MOBA_BLOCK)
            qt_ref[sb, s * LANES:(s + 1) * LANES, :] = qs[rows].T.astype(BF16)
    u_ref[...] = jax.nn.gelu(proj[:, 3 * aw:3 * aw + gw])
    gv = jax.nn.gelu(proj[:, 3 * aw + gw:3 * aw + 2 * gw])
    mu = jnp.mean(gv, axis=-1, keepdims=True)
    var = jnp.mean(jnp.square(gv - mu), axis=-1, keepdims=True)
    vln = (gv - mu) * lax.rsqrt(var + NORM_EPS) * lng_ref[...] + lnb_ref[...]
    vln_ref[...] = vln.astype(BF16)


def _rope_tables(seq):
    half = HEAD_DIM // 2
    pos = jnp.arange(seq, dtype=F32)
    inv_freq = ROPE_THETA ** (-jnp.arange(half, dtype=F32) / half)
    ang = pos[:, None] * inv_freq[None, :]
    cos, sin = jnp.cos(ang), jnp.sin(ang)
    reps = LANES // HEAD_DIM
    return (jnp.tile(jnp.concatenate([cos, cos], axis=-1), (1, reps)),
            jnp.tile(jnp.concatenate([-sin, sin], axis=-1), (1, reps)))


def _inproj(h, mod, norm_w, w_in, ln_g, ln_b, *, batch, seq, aw, gw, tm):
    n_tok, D = h.shape
    n_heads = aw // HEAD_DIM
    st = seq // tm
    bpt = tm // MOBA_BLOCK
    n_blk = seq // MOBA_BLOCK
    tpg = KV_GROUP * MOBA_BLOCK // tm
    cos_t, sin_t = _rope_tables(seq)
    kern = functools.partial(_inproj_kernel, aw=aw, gw=gw, st=st)
    return pl.pallas_call(
        kern,
        grid=(n_tok // tm,),
        in_specs=[pl.BlockSpec((tm, D), lambda i: (i, 0)),
                  pl.BlockSpec((1, N_MOD, D), lambda i: (i // st, 0, 0)),
                  _resident((1, D)),
                  _resident((D, 3 * aw + 2 * gw)),
                  pl.BlockSpec((tm, LANES), lambda i: (i % st, 0)),
                  pl.BlockSpec((tm, LANES), lambda i: (i % st, 0)),
                  _resident((1, gw)),
                  _resident((1, gw))],
        out_specs=[pl.BlockSpec((None, bpt, aw, MOBA_BLOCK), lambda i: (i // st, i % st, 0, 0)),
                   pl.BlockSpec((None, n_heads, tm, LANES), lambda i: (i // st, 0, i % st, 0)),
                   pl.BlockSpec((None, None, aw, tm), lambda i: (i // st, (i % st) // tpg, 0, (i % st) % tpg)),
                   pl.BlockSpec((tm, gw), lambda i: (i, 0)),
                   pl.BlockSpec((tm, gw), lambda i: (i, 0))],
        out_shape=[jax.ShapeDtypeStruct((batch, n_blk, aw, MOBA_BLOCK), BF16),
                   jax.ShapeDtypeStruct((batch, n_heads, seq, LANES), BF16),
                   jax.ShapeDtypeStruct((batch, n_blk // KV_GROUP, aw, KV_GROUP * MOBA_BLOCK), BF16),
                   jax.ShapeDtypeStruct((n_tok, gw), F32),
                   jax.ShapeDtypeStruct((n_tok, gw), BF16)],
        compiler_params=_cparams(1),
        name="inproj",
    )(h, mod, norm_w.reshape(1, D), w_in.astype(BF16), cos_t, sin_t, ln_g.reshape(1, gw), ln_b.reshape(1, gw))


SUM_ROWS = 16


ATTN_HEADS_PER_STEP = 4


def _moba_kernel(qt_ref, qt_all_ref, k_ref, vt_ref, o_ref, km_ref, qtail_ref, s_ref, *, n_blk):
    i = pl.program_id(2)
    blk = MOBA_BLOCK
    grp = KV_GROUP * blk
    heads = range(ATTN_HEADS_PER_STEP)

    @pl.when(i == 0)
    def _():
        def mean_body(j, carry):
            for hh in heads:
                kj = k_ref[hh, pl.ds(pl.multiple_of(j * blk, blk), blk), :].astype(F32)
                km_ref[hh, pl.ds(j, 1), :] = jnp.mean(kj, axis=0, keepdims=True)
            return carry

        lax.fori_loop(0, n_blk, mean_body, 0)

        blk_id = lax.broadcasted_iota(jnp.int32, (n_blk, blk), 0)
        zeros_tail = jnp.zeros((LANES - HEAD_DIM, blk), BF16)
        bias_pad = jnp.zeros((LANES - HEAD_DIM - n_blk, blk), BF16)

        def gate_body(ib, carry):
            for hh in heads:
                qt = qt_all_ref[ib, hh * HEAD_DIM:(hh + 1) * HEAD_DIM, :]
                q_f32 = jnp.concatenate([qt, zeros_tail], axis=0).astype(F32)
                gate = jnp.dot(km_ref[hh], q_f32, precision=lax.Precision.HIGHEST, preferred_element_type=F32)
                gate = jnp.where(blk_id < ib, gate, NEG_INF)
                bias = jnp.where(blk_id == ib, 0.0, NEG_INF)
                for r in range(min(MOBA_TOPK, n_blk)):
                    top = jnp.max(gate, axis=0, keepdims=True)
                    idx = jnp.min(jnp.where(gate == top, blk_id, n_blk), axis=0, keepdims=True)
                    hit = blk_id == idx
                    rank_bias = jnp.where(r < ib, 0.0, NEG_INF)
                    bias = jnp.where(hit, jnp.maximum(bias, rank_bias), bias)
                    gate = jnp.where(hit, -jnp.inf, gate)
                qtail_ref[ib, hh] = jnp.concatenate([bias.astype(BF16), bias_pad], axis=0)
            return carry

        lax.fori_loop(0, n_blk, gate_body, 0)

    ones_rows = jnp.ones((SUM_ROWS, grp), BF16)
    q_aug = [jnp.concatenate([qt_ref[hh * HEAD_DIM:(hh + 1) * HEAD_DIM, :], qtail_ref[i, hh]], axis=0)
             for hh in heads]

    def score(u, slot):
        start = pl.multiple_of(u * grp, grp)
        maxes = []
        for hh in heads:
            s = jnp.dot(k_ref[hh, pl.ds(start, grp), :], q_aug[hh], preferred_element_type=F32)
            s_ref[slot, hh] = s
            maxes.append(jnp.max(s, axis=0, keepdims=True))
        return maxes

    def accumulate(u, slot, unit_max, m_run, acc):
        m_out, acc_out = [], []
        for hh in heads:
            m_new = jnp.maximum(m_run[hh], unit_max[hh])
            p = jnp.exp2(s_ref[slot, hh] - m_new).astype(BF16)
            vt = jnp.concatenate([vt_ref[u, hh * HEAD_DIM:(hh + 1) * HEAD_DIM, :], ones_rows], axis=0)
            pv = jnp.dot(vt, p, preferred_element_type=F32)
            acc_out.append(jnp.exp2(m_run[hh] - m_new) * acc[hh] + pv)
            m_out.append(m_new)
        return m_out, acc_out

    last = i // KV_GROUP
    own = pl.multiple_of((i % KV_GROUP) * blk, blk)
    key_pos = lax.broadcasted_iota(jnp.int32, (blk, blk), 0)
    q_pos = lax.broadcasted_iota(jnp.int32, (blk, blk), 1)
    causal = key_pos <= q_pos
    score(last, 0)
    unit_max = []
    for hh in heads:
        s_ref[0, hh, pl.ds(own, blk), :] = jnp.where(causal, s_ref[0, hh, pl.ds(own, blk), :], NEG_INF)
        unit_max.append(jnp.max(s_ref[0, hh], axis=0, keepdims=True))

    m_run = [jnp.full((1, blk), NEG_INF, F32) for _ in heads]
    acc = [jnp.zeros((HEAD_DIM + SUM_ROWS, blk), F32) for _ in heads]
    n_h = ATTN_HEADS_PER_STEP

    def body(n, carry):
        unit_max, m_run, acc = carry[:n_h], carry[n_h:2 * n_h], carry[2 * n_h:]
        u = last - n
        slot = n % 2
        m_run, acc = accumulate(u, slot, unit_max, m_run, acc)
        unit_max = score(u - 1, 1 - slot)
        return tuple(unit_max) + tuple(m_run) + tuple(acc)

    carry = lax.fori_loop(0, last, body, tuple(unit_max) + tuple(m_run) + tuple(acc))
    unit_max, m_run, acc = carry[:n_h], carry[n_h:2 * n_h], carry[2 * n_h:]
    _, acc = accumulate(0, last % 2, unit_max, m_run, acc)
    outs = [a[:HEAD_DIM] * (1.0 / a[HEAD_DIM:HEAD_DIM + 1]) for a in acc]
    o_ref[...] = jnp.concatenate(outs, axis=0).T


def _moba(qt, k, vt, *, batch, seq, aw):
    n_blk = seq // MOBA_BLOCK
    n_grp = n_blk // KV_GROUP
    n_h = ATTN_HEADS_PER_STEP
    width = n_h * HEAD_DIM
    once = pl.Buffered(1)
    kern = functools.partial(_moba_kernel, n_blk=n_blk)
    return pl.pallas_call(
        kern,
        grid=(batch, aw // width, n_blk),
        in_specs=[pl.BlockSpec((None, None, width, MOBA_BLOCK), lambda b, hg, i: (b, i, hg, 0)),
                  pl.BlockSpec((None, n_blk, width, MOBA_BLOCK), lambda b, hg, i: (b, 0, hg, 0),
                               pipeline_mode=once),
                  pl.BlockSpec((None, n_h, seq, LANES), lambda b, hg, i: (b, hg, 0, 0), pipeline_mode=once),
                  pl.BlockSpec((None, n_grp, width, KV_GROUP * MOBA_BLOCK), lambda b, hg, i: (b, 0, hg, 0),
                               pipeline_mode=once)],
        out_specs=pl.BlockSpec((None, MOBA_BLOCK, width), lambda b, hg, i: (b, i, hg)),
        out_shape=jax.ShapeDtypeStruct((batch, seq, aw), F32),
        scratch_shapes=[pltpu.VMEM((n_h, n_blk, LANES), F32),
                        pltpu.VMEM((n_blk, n_h, LANES - HEAD_DIM, MOBA_BLOCK), BF16),
                        pltpu.VMEM((2, n_h, KV_GROUP * MOBA_BLOCK, MOBA_BLOCK), F32)],
        compiler_params=_cparams(3),
        name="moba",
    )(qt, qt, k, vt)


def _mixout_kernel(h_ref, mod_ref, a_ref, u_ref, vln_ref, ws_ref, bias_ref, ga_ref, gg_ref, wout_ref, o_ref,
                   *, aw, gw, chunk):
    tm = h_ref.shape[0]
    gt = mod_ref[0, 5:6, :]
    attn_n = _rmsnorm(a_ref[...], ga_ref[...]).astype(BF16)

    n_grp = gw // LANES
    row = lax.broadcasted_iota(jnp.int32, (HEADS_PER_STEP * chunk, chunk), 0)
    col = lax.broadcasted_iota(jnp.int32, (HEADS_PER_STEP * chunk, chunk), 1)
    keep = (row % chunk) >= col
    w_pairs = [jnp.where(keep, ws_ref[g], 0.0).astype(BF16) for g in range(n_grp)]
    lane = lax.broadcasted_iota(jnp.int32, (chunk, LANES), 1)
    first_head = lane < HEAD_DIM
    bias = bias_ref[...]

    rows = []
    for c in range(tm // chunk):
        vc = vln_ref[c * chunk:(c + 1) * chunk, :]
        mixed = []
        for g in range(n_grp):
            r = jnp.dot(w_pairs[g], vc[:, g * LANES:(g + 1) * LANES], preferred_element_type=F32)
            mixed.append(jnp.where(first_head, r[:chunk], r[chunk:]))
        mixed = jnp.concatenate(mixed, axis=1) + bias
        rows.append(u_ref[c * chunk:(c + 1) * chunk, :] * mixed)
    gm = jnp.concatenate(rows, axis=0)
    gmlp_n = _rmsnorm(gm, gg_ref[...]).astype(BF16)

    y = jnp.dot(attn_n, wout_ref[:aw, :], preferred_element_type=F32)
    y = y + jnp.dot(gmlp_n, wout_ref[aw:, :], preferred_element_type=F32)
    o_ref[...] = h_ref[...] + gt * y


def _mixout(h, mod, attn, u, vln, w_s, b_s, g_attn, g_gmlp, w_out, *, seq, tm):
    n_tok, D = h.shape
    aw, gw = g_attn.shape[0], g_gmlp.shape[0]
    n_gh, chunk = w_s.shape[0], w_s.shape[1]
    ws_pairs = w_s.reshape(n_gh // HEADS_PER_STEP, HEADS_PER_STEP * chunk, chunk)
    bias = jnp.repeat(b_s.T, gw // n_gh, axis=1)
    kern = functools.partial(_mixout_kernel, aw=aw, gw=gw, chunk=chunk)
    return pl.pallas_call(
        kern,
        grid=(n_tok // tm,),
        in_specs=[pl.BlockSpec((tm, D), lambda i: (i, 0)),
                  pl.BlockSpec((1, N_MOD, D), lambda i: (i * tm // seq, 0, 0)),
                  pl.BlockSpec((tm, aw), lambda i: (i, 0)),
                  pl.BlockSpec((tm, gw), lambda i: (i, 0)),
                  pl.BlockSpec((tm, gw), lambda i: (i, 0)),
                  _resident(ws_pairs.shape),
                  _resident((chunk, gw)),
                  _resident((1, aw)),
                  _resident((1, gw)),
                  _resident((aw + gw, D))],
        out_specs=pl.BlockSpec((tm, D), lambda i: (i, 0)),
        out_shape=jax.ShapeDtypeStruct((n_tok, D), F32),
        compiler_params=_cparams(1),
        name="mixout",
    )(h, mod, attn, u, vln, ws_pairs, bias, g_attn.reshape(1, aw), g_gmlp.reshape(1, gw), w_out.astype(BF16))


def kernel(x, c, w_ada, b_ada, norm_ffn1, w_ffn1_gu, w_ffn1_down, norm_mix, w_in, gmlp_ln_g, gmlp_ln_b,
           gmlp_w_s, gmlp_b_s, g_attn_out, g_gmlp_out, w_out, norm_ffn2, w_ffn2_gu, w_ffn2_down, norm_final):
    B, S, D = x.shape
    depth = w_ada.shape[0]
    aw, gw = g_attn_out.shape[1], g_gmlp_out.shape[1]
    assert S % (KV_GROUP * MOBA_BLOCK) == 0 and aw % (ATTN_HEADS_PER_STEP * HEAD_DIM) == 0 and gw % LANES == 0
    assert S // MOBA_BLOCK <= LANES - HEAD_DIM
    tm = 512
    h = x.reshape(B * S, D)
    for l in range(depth):
        last = l == depth - 1
        mod = _ada(c, w_ada[l], b_ada[l])
        h = _ffn(h, mod, norm_ffn1[l], w_ffn1_gu[l], w_ffn1_down[l], norm_final,
                 seq=S, mod_base=0, final_norm=False, tm=tm)
        qt, k, vt, u, vln = _inproj(h, mod, norm_mix[l], w_in[l], gmlp_ln_g[l], gmlp_ln_b[l],
                                    batch=B, seq=S, aw=aw, gw=gw, tm=tm)
        attn = _moba(qt, k, vt, batch=B, seq=S, aw=aw).reshape(B * S, aw)
        h = _mixout(h, mod, attn, u, vln, gmlp_w_s[l], gmlp_b_s[l], g_attn_out[l], g_gmlp_out[l], w_out[l],
                    seq=S, tm=tm)
        h = _ffn(h, mod, norm_ffn2[l], w_ffn2_gu[l], w_ffn2_down[l], norm_final,
                 seq=S, mod_base=6, final_norm=last, tm=tm)
    return h.reshape(B, S, D)
```

```python
import functools

import jax
import jax.numpy as jnp
import numpy as np
from jax import lax
from jax.experimental import pallas as pl
from jax.experimental.pallas import tpu as pltpu

HEAD_DIM = 64
MOBA_BLOCK = 256
MOBA_TOPK = 3
ROPE_THETA = 10000.0
NORM_EPS = 1e-6
N_MOD = 9
NEG_INF = -1e30
LANES = 128
HEADS_PER_SLAB = LANES // HEAD_DIM
ATTN_HEADS_PER_STEP = 4
ATTN_UNROLL = 14
SUM_ROWS = 16
KM_ROWS = 8
VMEM_LIMIT_BYTES = 56 * 2**20

F32 = jnp.float32
BF16 = jnp.bfloat16


def _cparams(n_axes):
    return pltpu.CompilerParams(dimension_semantics=("arbitrary",) * n_axes,
                                vmem_limit_bytes=VMEM_LIMIT_BYTES)


def _resident(shape):
    nd = len(shape)
    return pl.BlockSpec(shape, lambda *_: (0,) * nd, pipeline_mode=pl.Buffered(1))


def _rmsnorm(x, g):
    return x * lax.rsqrt(jnp.mean(x * x, axis=-1, keepdims=True) + NORM_EPS) * g


def _ada_kernel(c_ref, w_ref, b_ref, o_ref):
    ca = jax.nn.silu(c_ref[...])
    o_ref[...] = jnp.dot(ca, w_ref[...], precision=lax.Precision.HIGHEST,
                         preferred_element_type=F32) + b_ref[...]


def _ada(c, w_ada, b_ada):
    B, D = c.shape
    n_out = w_ada.shape[1]
    rows = -(-B // 8) * 8
    c_pad = jnp.pad(c, ((0, rows - B), (0, 0)))
    tn = n_out // 8 if (n_out // 8) % LANES == 0 else n_out
    mod = pl.pallas_call(
        _ada_kernel,
        grid=(n_out // tn,),
        in_specs=[pl.BlockSpec((rows, D), lambda j: (0, 0)),
                  pl.BlockSpec((D, tn), lambda j: (0, j)),
                  pl.BlockSpec((1, tn), lambda j: (0, j))],
        out_specs=pl.BlockSpec((rows, tn), lambda j: (0, j)),
        out_shape=jax.ShapeDtypeStruct((rows, n_out), F32),
        compiler_params=_cparams(1),
        name="ada",
    )(c_pad, w_ada, b_ada.reshape(1, n_out))
    return mod[:B].reshape(B, N_MOD, D)


def _ffn_kernel(h_ref, mod_ref, nw_ref, wgu_ref, wd_ref, nf_ref, o_ref, *, mod_base, d_ff, tf, final_norm):
    x = h_ref[...]
    sh = mod_ref[0, mod_base:mod_base + 1, :]
    sc = mod_ref[0, mod_base + 1:mod_base + 2, :]
    gt = mod_ref[0, mod_base + 2:mod_base + 3, :]
    y = (_rmsnorm(x, nw_ref[...]) * (1.0 + sc) + sh).astype(BF16)
    acc = jnp.zeros(x.shape, F32)
    for c in range(d_ff // tf):
        g = jnp.dot(y, wgu_ref[:, c * tf:(c + 1) * tf], preferred_element_type=F32)
        u = jnp.dot(y, wgu_ref[:, d_ff + c * tf:d_ff + (c + 1) * tf], preferred_element_type=F32)
        mid = (jax.nn.silu(g) * u).astype(BF16)
        acc = acc + jnp.dot(mid, wd_ref[c * tf:(c + 1) * tf, :], preferred_element_type=F32)
    out = x + (0.5 * gt) * acc
    if final_norm:
        out = _rmsnorm(out, nf_ref[...])
    o_ref[...] = out


def _ffn(h, mod, norm_w, w_gu, w_down, norm_final, *, seq, mod_base, final_norm, tm):
    n_tok, D = h.shape
    d_ff = w_down.shape[0]
    tf = 256 if d_ff % 256 == 0 else d_ff
    kern = functools.partial(_ffn_kernel, mod_base=mod_base, d_ff=d_ff, tf=tf, final_norm=final_norm)
    return pl.pallas_call(
        kern,
        grid=(n_tok // tm,),
        in_specs=[pl.BlockSpec((tm, D), lambda i: (i, 0)),
                  pl.BlockSpec((1, N_MOD, D), lambda i: (i * tm // seq, 0, 0)),
                  _resident((1, D)),
                  _resident((D, 2 * d_ff)),
                  _resident((d_ff, D)),
                  _resident((1, D))],
        out_specs=pl.BlockSpec((tm, D), lambda i: (i, 0)),
        out_shape=jax.ShapeDtypeStruct((n_tok, D), F32),
        compiler_params=_cparams(1),
        name="ffn_final" if final_norm else "ffn",
    )(h, mod, norm_w.reshape(1, D), w_gu.astype(BF16), w_down.astype(BF16), norm_final.reshape(1, D))


def _rope(x, cos, sin_signed, first_half):
    partner = jnp.where(first_half, pltpu.roll(x, LANES - HEAD_DIM // 2, 1), pltpu.roll(x, HEAD_DIM // 2, 1))
    return x * cos + partner * sin_signed


def _inproj_kernel(h_ref, mod_ref, nw_ref, win_ref, cos_ref, sin_ref, lng_ref, lnb_ref,
                   qt_ref, k_ref, vt_ref, km_ref, u_ref, vln_ref, *, aw, gw, st):
    x = h_ref[...]
    sh = mod_ref[0, 3:4, :]
    sc = mod_ref[0, 4:5, :]
    y = (_rmsnorm(x, nw_ref[...]) * (1.0 + sc) + sh).astype(BF16)
    proj = jnp.dot(y, win_ref[...], preferred_element_type=F32)
    cos = cos_ref[...]
    sin_signed = sin_ref[...]
    tm = x.shape[0]
    bpt = tm // MOBA_BLOCK
    lane = lax.broadcasted_iota(jnp.int32, cos.shape, 1)
    first_half = (lane % HEAD_DIM) < (HEAD_DIM // 2)
    is_key_lane = lane < HEAD_DIM
    row = lax.broadcasted_iota(jnp.int32, cos.shape, 0)
    blk_of_row = (pl.program_id(0) % st) * bpt + row // MOBA_BLOCK
    blk_onehot = jnp.where(lane - HEAD_DIM == blk_of_row, 1.0, 0.0)
    scale = HEAD_DIM ** -0.5 * np.log2(np.e)
    km_ref[...] = jnp.zeros(km_ref.shape, F32)
    for s in range(aw // LANES):
        qs = _rope(proj[:, s * LANES:(s + 1) * LANES], cos, sin_signed, first_half) * scale
        ks = _rope(proj[:, aw + s * LANES:aw + (s + 1) * LANES], cos, sin_signed, first_half)
        k_ref[2 * s] = jnp.where(is_key_lane, ks, blk_onehot).astype(BF16)
        k_ref[2 * s + 1] = jnp.where(is_key_lane, pltpu.roll(ks, HEAD_DIM, 1), blk_onehot).astype(BF16)
        vs = proj[:, 2 * aw + s * LANES:2 * aw + (s + 1) * LANES]
        for sb in range(bpt):
            rows = slice(sb * MOBA_BLOCK, (sb + 1) * MOBA_BLOCK)
            km_ref[sb:sb + 1, s * LANES:(s + 1) * LANES] = jnp.mean(ks[rows], axis=0, keepdims=True)
            qt_ref[sb, s * LANES:(s + 1) * LANES, :] = qs[rows].T.astype(BF16)
            vt_ref[sb, s * LANES:(s + 1) * LANES, :] = vs[rows].T.astype(BF16)
    u_ref[...] = jax.nn.gelu(proj[:, 3 * aw:3 * aw + gw])
    gv = jax.nn.gelu(proj[:, 3 * aw + gw:3 * aw + 2 * gw])
    mu = jnp.mean(gv, axis=-1, keepdims=True)
    var = jnp.mean(jnp.square(gv - mu), axis=-1, keepdims=True)
    vln = (gv - mu) * lax.rsqrt(var + NORM_EPS) * lng_ref[...] + lnb_ref[...]
    vln_ref[...] = vln.astype(BF16)


def _rope_tables(seq):
    half = HEAD_DIM // 2
    pos = jnp.arange(seq, dtype=F32)
    inv_freq = ROPE_THETA ** (-jnp.arange(half, dtype=F32) / half)
    ang = pos[:, None] * inv_freq[None, :]
    cos, sin = jnp.cos(ang), jnp.sin(ang)
    return (jnp.tile(jnp.concatenate([cos, cos], axis=-1), (1, HEADS_PER_SLAB)),
            jnp.tile(jnp.concatenate([-sin, sin], axis=-1), (1, HEADS_PER_SLAB)))


def _inproj(h, mod, norm_w, w_in, ln_g, ln_b, *, batch, seq, aw, gw, tm):
    n_tok, D = h.shape
    n_heads = aw // HEAD_DIM
    st = seq // tm
    bpt = tm // MOBA_BLOCK
    n_blk = seq // MOBA_BLOCK
    cos_t, sin_t = _rope_tables(seq)
    kern = functools.partial(_inproj_kernel, aw=aw, gw=gw, st=st)
    qt, k, vt, km, u, vln = pl.pallas_call(
        kern,
        grid=(n_tok // tm,),
        in_specs=[pl.BlockSpec((tm, D), lambda i: (i, 0)),
                  pl.BlockSpec((1, N_MOD, D), lambda i: (i // st, 0, 0)),
                  _resident((1, D)),
                  _resident((D, 3 * aw + 2 * gw)),
                  pl.BlockSpec((tm, LANES), lambda i: (i % st, 0)),
                  pl.BlockSpec((tm, LANES), lambda i: (i % st, 0)),
                  _resident((1, gw)),
                  _resident((1, gw))],
        out_specs=[pl.BlockSpec((None, bpt, aw, MOBA_BLOCK), lambda i: (i // st, i % st, 0, 0)),
                   pl.BlockSpec((None, n_heads, tm, LANES), lambda i: (i // st, 0, i % st, 0)),
                   pl.BlockSpec((None, bpt, aw, MOBA_BLOCK), lambda i: (i // st, i % st, 0, 0)),
                   pl.BlockSpec((None, None, KM_ROWS, aw), lambda i: (i // st, i % st, 0, 0)),
                   pl.BlockSpec((tm, gw), lambda i: (i, 0)),
                   pl.BlockSpec((tm, gw), lambda i: (i, 0))],
        out_shape=[jax.ShapeDtypeStruct((batch, n_blk, aw, MOBA_BLOCK), BF16),
                   jax.ShapeDtypeStruct((batch, n_heads, seq, LANES), BF16),
                   jax.ShapeDtypeStruct((batch, n_blk, aw, MOBA_BLOCK), BF16),
                   jax.ShapeDtypeStruct((batch, st, KM_ROWS, aw), F32),
                   jax.ShapeDtypeStruct((n_tok, gw), F32),
                   jax.ShapeDtypeStruct((n_tok, gw), BF16)],
        compiler_params=_cparams(1),
        name="inproj",
    )(h, mod, norm_w.reshape(1, D), w_in.astype(BF16), cos_t, sin_t, ln_g.reshape(1, gw), ln_b.reshape(1, gw))
    km = km[:, :, :bpt, :].reshape(batch, n_blk, aw)
    return qt, k, vt, km, u, vln


def _moba_kernel(qa_ref, qb_ref, qt_all_ref, km_ref, k_ref, vt_ref, o_ref, qtail_ref, qaug_ref, s_ref, cmax_ref,
                 *state_refs, n_blk):
    p = pl.program_id(2)
    blk = MOBA_BLOCK
    n_h = ATTN_HEADS_PER_STEP
    heads = range(n_h)
    m_refs, acc_refs = state_refs[:n_h], state_refs[n_h:]
    zeros_half = jnp.zeros((HEAD_DIM, blk), BF16)

    @pl.when(p == 0)
    def _():
        blk_id = lax.broadcasted_iota(jnp.int32, (n_blk, blk), 0)
        bias_pad = jnp.zeros((LANES - HEAD_DIM - n_blk, blk), BF16)

        def gate_body(ib, carry):
            for hh in heads:
                qt = qt_all_ref[ib, hh * HEAD_DIM:(hh + 1) * HEAD_DIM, :]
                halves = [zeros_half, qt] if hh % HEADS_PER_SLAB else [qt, zeros_half]
                q_f32 = jnp.concatenate(halves, axis=0).astype(F32)
                slab = hh // HEADS_PER_SLAB
                gate = jnp.dot(km_ref[:, slab * LANES:(slab + 1) * LANES], q_f32,
                               precision=lax.Precision.HIGHEST, preferred_element_type=F32)
                gate = jnp.where(blk_id < ib, gate, NEG_INF)
                bias = jnp.where(blk_id == ib, 0.0, NEG_INF)
                for r in range(min(MOBA_TOPK, n_blk)):
                    top = jnp.max(gate, axis=0, keepdims=True)
                    idx = jnp.min(jnp.where(gate == top, blk_id, n_blk), axis=0, keepdims=True)
                    hit = blk_id == idx
                    rank_bias = jnp.where(r < ib, 0.0, NEG_INF)
                    bias = jnp.where(hit, jnp.maximum(bias, rank_bias), bias)
                    gate = jnp.where(hit, -jnp.inf, gate)
                qtail_ref[ib, hh] = jnp.concatenate([bias.astype(BF16), bias_pad], axis=0)
            return carry

        lax.fori_loop(0, n_blk, gate_body, 0)

    blk_a, blk_b = p, n_blk - 1 - p
    ones_rows = jnp.ones((SUM_ROWS, blk), BF16)
    key_pos = lax.broadcasted_iota(jnp.int32, (blk, blk), 0)
    q_pos = lax.broadcasted_iota(jnp.int32, (blk, blk), 1)
    causal = key_pos <= q_pos

    for slot, (q_ref, own) in enumerate(((qb_ref, blk_b), (qa_ref, blk_a))):
        for hh in heads:
            qaug_ref[slot, hh] = jnp.concatenate(
                [q_ref[hh * HEAD_DIM:(hh + 1) * HEAD_DIM, :], qtail_ref[own, hh]], axis=0)
            m_refs[hh][slot] = jnp.full((1, blk), NEG_INF, F32)
            acc_refs[hh][slot] = jnp.zeros((HEAD_DIM + SUM_ROWS, blk), F32)

    def unit(t):
        n = t - 2
        slot = jnp.where(t < 2, t, jnp.where(n >= blk_b, 1, 0))
        j = jnp.where(t == 0, blk_b, jnp.where(t == 1, blk_a, jnp.where(n >= blk_b, n - blk_b, n)))
        return slot, j

    def score(slot, j, buf, own):
        start = pl.multiple_of(j * blk, blk)
        for hh in heads:
            s = jnp.dot(k_ref[hh, pl.ds(start, blk), :], qaug_ref[slot, hh], preferred_element_type=F32)
            if own:
                s = jnp.where(causal, s, NEG_INF)
            s_ref[buf, hh] = s
            cmax_ref[buf, hh] = jnp.max(s, axis=0, keepdims=True)

    def accumulate(slot, j, buf):
        for hh in heads:
            m_old = m_refs[hh][slot]
            m_new = jnp.maximum(m_old, cmax_ref[buf, hh])
            e = jnp.exp2(s_ref[buf, hh] - m_new).astype(BF16)
            vt = jnp.concatenate([vt_ref[j, hh * HEAD_DIM:(hh + 1) * HEAD_DIM, :], ones_rows], axis=0)
            pv = jnp.dot(vt, e, preferred_element_type=F32)
            m_refs[hh][slot] = m_new
            acc_refs[hh][slot] = jnp.exp2(m_old - m_new) * acc_refs[hh][slot] + pv

    n_units = n_blk + 1
    score(*unit(0), 0, True)
    score(*unit(1), 1, True)

    def stages(t, buf):
        accumulate(*unit(t), buf)
        score(*unit(t + 2), buf, False)

    n_loop = (n_units - 2) // ATTN_UNROLL * ATTN_UNROLL

    def body(it, carry):
        for k in range(ATTN_UNROLL):
            stages(it * ATTN_UNROLL + k, k % 2)
        return carry

    lax.fori_loop(0, n_loop // ATTN_UNROLL, body, 0)
    for t in range(n_loop, n_units - 2):
        stages(t, t % 2)
    for t in range(n_units - 2, n_units):
        accumulate(*unit(t), t % 2)

    for slot in range(2):
        outs = []
        for hh in heads:
            acc = acc_refs[hh][slot]
            outs.append(acc[:HEAD_DIM] * (1.0 / acc[HEAD_DIM:HEAD_DIM + 1]))
        o_ref[1 - slot] = jnp.concatenate(outs, axis=0).T


def _moba(qt, k, vt, km, *, batch, seq, aw):
    n_blk = seq // MOBA_BLOCK
    n_h = ATTN_HEADS_PER_STEP
    width = n_h * HEAD_DIM
    kern = functools.partial(_moba_kernel, n_blk=n_blk)
    return pl.pallas_call(
        kern,
        grid=(batch, aw // width, n_blk // 2),
        in_specs=[pl.BlockSpec((None, None, width, MOBA_BLOCK), lambda b, hg, p: (b, p, hg, 0)),
                  pl.BlockSpec((None, None, width, MOBA_BLOCK), lambda b, hg, p: (b, n_blk - 1 - p, hg, 0)),
                  pl.BlockSpec((None, n_blk, width, MOBA_BLOCK), lambda b, hg, p: (b, 0, hg, 0)),
                  pl.BlockSpec((None, n_blk, width), lambda b, hg, p: (b, 0, hg)),
                  pl.BlockSpec((None, n_h, seq, LANES), lambda b, hg, p: (b, hg, 0, 0)),
                  pl.BlockSpec((None, n_blk, width, MOBA_BLOCK), lambda b, hg, p: (b, 0, hg, 0))],
        out_specs=pl.BlockSpec((None, 2, None, MOBA_BLOCK, width), lambda b, hg, p: (b, 0, p, 0, hg)),
        out_shape=jax.ShapeDtypeStruct((batch, 2, n_blk // 2, MOBA_BLOCK, aw), F32),
        scratch_shapes=[pltpu.VMEM((n_blk, n_h, LANES - HEAD_DIM, MOBA_BLOCK), BF16),
                        pltpu.VMEM((2, n_h, LANES, MOBA_BLOCK), BF16),
                        pltpu.VMEM((2, n_h, MOBA_BLOCK, MOBA_BLOCK), F32),
                        pltpu.VMEM((2, n_h, 1, MOBA_BLOCK), F32)]
        + [pltpu.VMEM((2, 1, MOBA_BLOCK), F32)] * n_h
        + [pltpu.VMEM((2, HEAD_DIM + SUM_ROWS, MOBA_BLOCK), F32)] * n_h,
        compiler_params=_cparams(3),
        name="moba",
    )(qt, qt, qt, km, k, vt)


def _mixout_kernel(h_ref, mod_ref, a_lo_ref, a_hi_ref, u_ref, vln_ref, ws_ref, bias_ref, ga_ref, gg_ref, wout_ref,
                   o_ref, *, aw, gw, chunk):
    tm = h_ref.shape[0]
    gt = mod_ref[0, 5:6, :]
    attn = jnp.concatenate([a_lo_ref[...], a_hi_ref[...]], axis=0)
    attn_n = _rmsnorm(attn, ga_ref[...]).astype(BF16)

    n_grp = gw // LANES
    row = lax.broadcasted_iota(jnp.int32, (HEADS_PER_SLAB * chunk, chunk), 0)
    col = lax.broadcasted_iota(jnp.int32, (HEADS_PER_SLAB * chunk, chunk), 1)
    keep = (row % chunk) >= col
    w_pairs = [jnp.where(keep, ws_ref[g], 0.0).astype(BF16) for g in range(n_grp)]
    lane = lax.broadcasted_iota(jnp.int32, (chunk, LANES), 1)
    first_head = lane < HEAD_DIM
    bias = bias_ref[...]

    rows = []
    for c in range(tm // chunk):
        vc = vln_ref[c * chunk:(c + 1) * chunk, :]
        mixed = []
        for g in range(n_grp):
            r = jnp.dot(w_pairs[g], vc[:, g * LANES:(g + 1) * LANES], preferred_element_type=F32)
            mixed.append(jnp.where(first_head, r[:chunk], r[chunk:]))
        mixed = jnp.concatenate(mixed, axis=1) + bias
        rows.append(u_ref[c * chunk:(c + 1) * chunk, :] * mixed)
    gm = jnp.concatenate(rows, axis=0)
    gmlp_n = _rmsnorm(gm, gg_ref[...]).astype(BF16)

    y = jnp.dot(attn_n, wout_ref[:aw, :], preferred_element_type=F32)
    y = y + jnp.dot(gmlp_n, wout_ref[aw:, :], preferred_element_type=F32)
    o_ref[...] = h_ref[...] + gt * y


def _mixout(h, mod, attn, u, vln, w_s, b_s, g_attn, g_gmlp, w_out, *, seq, tm):
    n_tok, D = h.shape
    aw, gw = g_attn.shape[0], g_gmlp.shape[0]
    n_gh, chunk = w_s.shape[0], w_s.shape[1]
    assert tm == 2 * MOBA_BLOCK
    st = seq // tm
    half = seq // MOBA_BLOCK // 2
    ws_pairs = w_s.reshape(n_gh // HEADS_PER_SLAB, HEADS_PER_SLAB * chunk, chunk)
    bias = jnp.repeat(b_s.T, gw // n_gh, axis=1)

    def attn_spec(which):
        def index(i):
            j = (i % st) * 2 + which
            return (i // st, j // half, jnp.where(j < half, j, 2 * half - 1 - j), 0, 0)
        return pl.BlockSpec((None, None, None, MOBA_BLOCK, aw), index)

    kern = functools.partial(_mixout_kernel, aw=aw, gw=gw, chunk=chunk)
    return pl.pallas_call(
        kern,
        grid=(n_tok // tm,),
        in_specs=[pl.BlockSpec((tm, D), lambda i: (i, 0)),
                  pl.BlockSpec((1, N_MOD, D), lambda i: (i // st, 0, 0)),
                  attn_spec(0),
                  attn_spec(1),
                  pl.BlockSpec((tm, gw), lambda i: (i, 0)),
                  pl.BlockSpec((tm, gw), lambda i: (i, 0)),
                  _resident(ws_pairs.shape),
                  _resident((chunk, gw)),
                  _resident((1, aw)),
                  _resident((1, gw)),
                  _resident((aw + gw, D))],
        out_specs=pl.BlockSpec((tm, D), lambda i: (i, 0)),
        out_shape=jax.ShapeDtypeStruct((n_tok, D), F32),
        compiler_params=_cparams(1),
        name="mixout",
    )(h, mod, attn, attn, u, vln, ws_pairs, bias, g_attn.reshape(1, aw), g_gmlp.reshape(1, gw),
      w_out.astype(BF16))


def kernel(x, c, w_ada, b_ada, norm_ffn1, w_ffn1_gu, w_ffn1_down, norm_mix, w_in, gmlp_ln_g, gmlp_ln_b,
           gmlp_w_s, gmlp_b_s, g_attn_out, g_gmlp_out, w_out, norm_ffn2, w_ffn2_gu, w_ffn2_down, norm_final):
    B, S, D = x.shape
    depth = w_ada.shape[0]
    aw, gw = g_attn_out.shape[1], g_gmlp_out.shape[1]
    n_blk = S // MOBA_BLOCK
    assert S % (2 * MOBA_BLOCK) == 0 and aw % (ATTN_HEADS_PER_STEP * HEAD_DIM) == 0 and gw % LANES == 0
    assert n_blk <= LANES - HEAD_DIM and n_blk % 16 == 0
    tm = 2 * MOBA_BLOCK
    h = x.reshape(B * S, D)
    for l in range(depth):
        last = l == depth - 1
        mod = _ada(c, w_ada[l], b_ada[l])
        h = _ffn(h, mod, norm_ffn1[l], w_ffn1_gu[l], w_ffn1_down[l], norm_final,
                 seq=S, mod_base=0, final_norm=False, tm=tm)
        qt, k, vt, km, u, vln = _inproj(h, mod, norm_mix[l], w_in[l], gmlp_ln_g[l], gmlp_ln_b[l],
                                        batch=B, seq=S, aw=aw, gw=gw, tm=tm)
        attn = _moba(qt, k, vt, km, batch=B, seq=S, aw=aw)
        h = _mixout(h, mod, attn, u, vln, gmlp_w_s[l], gmlp_b_s[l], g_attn_out[l], g_gmlp_out[l], w_out[l],
                    seq=S, tm=tm)
        h = _ffn(h, mod, norm_ffn2[l], w_ffn2_gu[l], w_ffn2_down[l], norm_final,
                 seq=S, mod_base=6, final_norm=last, tm=tm)
    return h.reshape(B, S, D)
```

```python
import functools

import jax
import jax.numpy as jnp
import numpy as np
from jax import lax
from jax.experimental import pallas as pl
from jax.experimental.pallas import tpu as pltpu

HEAD_DIM = 64
MOBA_BLOCK = 256
MOBA_TOPK = 3
ROPE_THETA = 10000.0
NORM_EPS = 1e-6
N_MOD = 9
NEG_INF = -1e30
LANES = 128
HEADS_PER_SLAB = LANES // HEAD_DIM
ATTN_HEADS_PER_STEP = 4
ATTN_UNROLL = 14
SUM_ROWS = 16
KM_ROWS = 8
VMEM_LIMIT_BYTES = 56 * 2**20

F32 = jnp.float32
BF16 = jnp.bfloat16


def _cparams(n_axes):
    return pltpu.CompilerParams(dimension_semantics=("arbitrary",) * n_axes,
                                vmem_limit_bytes=VMEM_LIMIT_BYTES)


def _resident(shape):
    nd = len(shape)
    return pl.BlockSpec(shape, lambda *_: (0,) * nd, pipeline_mode=pl.Buffered(1))


def _rmsnorm(x, g):
    return x * lax.rsqrt(jnp.mean(x * x, axis=-1, keepdims=True) + NORM_EPS) * g


def _ada_kernel(c_ref, w_ref, b_ref, o_ref):
    ca = jax.nn.silu(c_ref[...])
    o_ref[...] = jnp.dot(ca, w_ref[...], precision=lax.Precision.HIGHEST,
                         preferred_element_type=F32) + b_ref[...]


def _ada(c, w_ada, b_ada):
    B, D = c.shape
    n_out = w_ada.shape[1]
    rows = -(-B // 8) * 8
    c_pad = jnp.pad(c, ((0, rows - B), (0, 0)))
    tn = n_out // 8 if (n_out // 8) % LANES == 0 else n_out
    mod = pl.pallas_call(
        _ada_kernel,
        grid=(n_out // tn,),
        in_specs=[pl.BlockSpec((rows, D), lambda j: (0, 0)),
                  pl.BlockSpec((D, tn), lambda j: (0, j)),
                  pl.BlockSpec((1, tn), lambda j: (0, j))],
        out_specs=pl.BlockSpec((rows, tn), lambda j: (0, j)),
        out_shape=jax.ShapeDtypeStruct((rows, n_out), F32),
        compiler_params=_cparams(1),
        name="ada",
    )(c_pad, w_ada, b_ada.reshape(1, n_out))
    return mod[:B].reshape(B, N_MOD, D)


def _ffn_kernel(h_ref, mod_ref, nw_ref, wgu_ref, wd_ref, nf_ref, o_ref, *, mod_base, d_ff, tf, final_norm):
    x = h_ref[...]
    sh = mod_ref[0, mod_base:mod_base + 1, :]
    sc = mod_ref[0, mod_base + 1:mod_base + 2, :]
    gt = mod_ref[0, mod_base + 2:mod_base + 3, :]
    y = (_rmsnorm(x, nw_ref[...]) * (1.0 + sc) + sh).astype(BF16)
    acc = jnp.zeros(x.shape, F32)
    for c in range(d_ff // tf):
        g = jnp.dot(y, wgu_ref[:, c * tf:(c + 1) * tf], preferred_element_type=F32)
        u = jnp.dot(y, wgu_ref[:, d_ff + c * tf:d_ff + (c + 1) * tf], preferred_element_type=F32)
        mid = (jax.nn.silu(g) * u).astype(BF16)
        acc = acc + jnp.dot(mid, wd_ref[c * tf:(c + 1) * tf, :], preferred_element_type=F32)
    out = x + (0.5 * gt) * acc
    if final_norm:
        out = _rmsnorm(out, nf_ref[...])
    o_ref[...] = out


def _ffn(h, mod, norm_w, w_gu, w_down, norm_final, *, seq, mod_base, final_norm, tm):
    n_tok, D = h.shape
    d_ff = w_down.shape[0]
    tf = 256 if d_ff % 256 == 0 else d_ff
    kern = functools.partial(_ffn_kernel, mod_base=mod_base, d_ff=d_ff, tf=tf, final_norm=final_norm)
    return pl.pallas_call(
        kern,
        grid=(n_tok // tm,),
        in_specs=[pl.BlockSpec((tm, D), lambda i: (i, 0)),
                  pl.BlockSpec((1, N_MOD, D), lambda i: (i * tm // seq, 0, 0)),
                  _resident((1, D)),
                  _resident((D, 2 * d_ff)),
                  _resident((d_ff, D)),
                  _resident((1, D))],
        out_specs=pl.BlockSpec((tm, D), lambda i: (i, 0)),
        out_shape=jax.ShapeDtypeStruct((n_tok, D), F32),
        compiler_params=_cparams(1),
        name="ffn_final" if final_norm else "ffn",
    )(h, mod, norm_w.reshape(1, D), w_gu.astype(BF16), w_down.astype(BF16), norm_final.reshape(1, D))


def _rope(x, cos, sin_signed, first_half):
    partner = jnp.where(first_half, pltpu.roll(x, LANES - HEAD_DIM // 2, 1), pltpu.roll(x, HEAD_DIM // 2, 1))
    return x * cos + partner * sin_signed


def _inproj_kernel(h_ref, mod_ref, nw_ref, win_ref, cos_ref, sin_ref, lng_ref, lnb_ref,
                   qt_ref, k_ref, vt_ref, km_ref, u_ref, vln_ref, *, aw, gw, st):
    x = h_ref[...]
    sh = mod_ref[0, 3:4, :]
    sc = mod_ref[0, 4:5, :]
    y = (_rmsnorm(x, nw_ref[...]) * (1.0 + sc) + sh).astype(BF16)
    proj = jnp.dot(y, win_ref[...], preferred_element_type=F32)
    cos = cos_ref[...]
    sin_signed = sin_ref[...]
    tm = x.shape[0]
    bpt = tm // MOBA_BLOCK
    lane = lax.broadcasted_iota(jnp.int32, cos.shape, 1)
    first_half = (lane % HEAD_DIM) < (HEAD_DIM // 2)
    is_key_lane = lane < HEAD_DIM
    row = lax.broadcasted_iota(jnp.int32, cos.shape, 0)
    blk_of_row = (pl.program_id(0) % st) * bpt + row // MOBA_BLOCK
    blk_onehot = jnp.where(lane - HEAD_DIM == blk_of_row, 1.0, 0.0)
    scale = HEAD_DIM ** -0.5 * np.log2(np.e)
    km_ref[...] = jnp.zeros(km_ref.shape, F32)
    for s in range(aw // LANES):
        qs = _rope(proj[:, s * LANES:(s + 1) * LANES], cos, sin_signed, first_half) * scale
        ks = _rope(proj[:, aw + s * LANES:aw + (s + 1) * LANES], cos, sin_signed, first_half)
        k_ref[2 * s] = jnp.where(is_key_lane, ks, blk_onehot).astype(BF16)
        k_ref[2 * s + 1] = jnp.where(is_key_lane, pltpu.roll(ks, HEAD_DIM, 1), blk_onehot).astype(BF16)
        vs = proj[:, 2 * aw + s * LANES:2 * aw + (s + 1) * LANES]
        for sb in range(bpt):
            rows = slice(sb * MOBA_BLOCK, (sb + 1) * MOBA_BLOCK)
            km_ref[sb:sb + 1, s * LANES:(s + 1) * LANES] = jnp.mean(ks[rows], axis=0, keepdims=True)
            qt_ref[sb, s * LANES:(s + 1) * LANES, :] = qs[rows].T.astype(BF16)
            vt_ref[sb, s * LANES:(s + 1) * LANES, :] = vs[rows].T.astype(BF16)
    u_ref[...] = jax.nn.gelu(proj[:, 3 * aw:3 * aw + gw]).astype(BF16)
    gv = jax.nn.gelu(proj[:, 3 * aw + gw:3 * aw + 2 * gw])
    mu = jnp.mean(gv, axis=-1, keepdims=True)
    var = jnp.mean(jnp.square(gv - mu), axis=-1, keepdims=True)
    vln = (gv - mu) * lax.rsqrt(var + NORM_EPS) * lng_ref[...] + lnb_ref[...]
    vln_ref[...] = vln.astype(BF16)


def _rope_tables(seq):
    half = HEAD_DIM // 2
    pos = jnp.arange(seq, dtype=F32)
    inv_freq = ROPE_THETA ** (-jnp.arange(half, dtype=F32) / half)
    ang = pos[:, None] * inv_freq[None, :]
    cos, sin = jnp.cos(ang), jnp.sin(ang)
    return (jnp.tile(jnp.concatenate([cos, cos], axis=-1), (1, HEADS_PER_SLAB)),
            jnp.tile(jnp.concatenate([-sin, sin], axis=-1), (1, HEADS_PER_SLAB)))


def _inproj(h, mod, norm_w, w_in, ln_g, ln_b, *, batch, seq, aw, gw, tm):
    n_tok, D = h.shape
    n_heads = aw // HEAD_DIM
    st = seq // tm
    bpt = tm // MOBA_BLOCK
    n_blk = seq // MOBA_BLOCK
    cos_t, sin_t = _rope_tables(seq)
    kern = functools.partial(_inproj_kernel, aw=aw, gw=gw, st=st)
    qt, k, vt, km, u, vln = pl.pallas_call(
        kern,
        grid=(n_tok // tm,),
        in_specs=[pl.BlockSpec((tm, D), lambda i: (i, 0)),
                  pl.BlockSpec((1, N_MOD, D), lambda i: (i // st, 0, 0)),
                  _resident((1, D)),
                  _resident((D, 3 * aw + 2 * gw)),
                  pl.BlockSpec((tm, LANES), lambda i: (i % st, 0)),
                  pl.BlockSpec((tm, LANES), lambda i: (i % st, 0)),
                  _resident((1, gw)),
                  _resident((1, gw))],
        out_specs=[pl.BlockSpec((None, bpt, aw, MOBA_BLOCK), lambda i: (i // st, i % st, 0, 0)),
                   pl.BlockSpec((None, n_heads, tm, LANES), lambda i: (i // st, 0, i % st, 0)),
                   pl.BlockSpec((None, bpt, aw, MOBA_BLOCK), lambda i: (i // st, i % st, 0, 0)),
                   pl.BlockSpec((None, None, KM_ROWS, aw), lambda i: (i // st, i % st, 0, 0)),
                   pl.BlockSpec((tm, gw), lambda i: (i, 0)),
                   pl.BlockSpec((tm, gw), lambda i: (i, 0))],
        out_shape=[jax.ShapeDtypeStruct((batch, n_blk, aw, MOBA_BLOCK), BF16),
                   jax.ShapeDtypeStruct((batch, n_heads, seq, LANES), BF16),
                   jax.ShapeDtypeStruct((batch, n_blk, aw, MOBA_BLOCK), BF16),
                   jax.ShapeDtypeStruct((batch, st, KM_ROWS, aw), F32),
                   jax.ShapeDtypeStruct((n_tok, gw), BF16),
                   jax.ShapeDtypeStruct((n_tok, gw), BF16)],
        compiler_params=_cparams(1),
        name="inproj",
    )(h, mod, norm_w.reshape(1, D), w_in.astype(BF16), cos_t, sin_t, ln_g.reshape(1, gw), ln_b.reshape(1, gw))
    km = km[:, :, :bpt, :].reshape(batch, n_blk, aw)
    return qt, k, vt, km, u, vln


def _moba_kernel(qa_ref, qb_ref, qt_all_ref, km_ref, k_ref, vt_ref, o_ref, qtail_ref, qaug_ref, s_ref, cmax_ref,
                 *state_refs, n_blk):
    p = pl.program_id(2)
    blk = MOBA_BLOCK
    n_h = ATTN_HEADS_PER_STEP
    heads = range(n_h)
    m_refs, acc_refs = state_refs[:n_h], state_refs[n_h:]
    zeros_half = jnp.zeros((HEAD_DIM, blk), BF16)

    @pl.when(p == 0)
    def _():
        blk_id = lax.broadcasted_iota(jnp.int32, (n_blk, blk), 0)
        bias_pad = jnp.zeros((LANES - HEAD_DIM - n_blk, blk), BF16)
        km_pieces = []
        for slab in range(n_h // HEADS_PER_SLAB):
            rest = km_ref[:, slab * LANES:(slab + 1) * LANES]
            pieces = []
            for _ in range(3):
                piece = rest.astype(BF16)
                rest = rest - piece.astype(F32)
                pieces.append(piece)
            km_pieces.append(pieces)

        def gate_body(ib, carry):
            for hh in heads:
                qt = qt_all_ref[ib, hh * HEAD_DIM:(hh + 1) * HEAD_DIM, :]
                halves = [zeros_half, qt] if hh % HEADS_PER_SLAB else [qt, zeros_half]
                q_slab = jnp.concatenate(halves, axis=0)
                gate = sum(jnp.dot(piece, q_slab, preferred_element_type=F32)
                           for piece in reversed(km_pieces[hh // HEADS_PER_SLAB]))
                gate = jnp.where(blk_id < ib, gate, NEG_INF)
                bias = jnp.where(blk_id == ib, 0.0, NEG_INF)
                for r in range(min(MOBA_TOPK, n_blk)):
                    top = jnp.max(gate, axis=0, keepdims=True)
                    idx = jnp.min(jnp.where(gate == top, blk_id, n_blk), axis=0, keepdims=True)
                    hit = blk_id == idx
                    rank_bias = jnp.where(r < ib, 0.0, NEG_INF)
                    bias = jnp.where(hit, jnp.maximum(bias, rank_bias), bias)
                    gate = jnp.where(hit, -jnp.inf, gate)
                qtail_ref[ib, hh] = jnp.concatenate([bias.astype(BF16), bias_pad], axis=0)
            return carry

        lax.fori_loop(0, n_blk, gate_body, 0)

    blk_a, blk_b = p, n_blk - 1 - p
    ones_rows = jnp.ones((SUM_ROWS, blk), BF16)
    key_pos = lax.broadcasted_iota(jnp.int32, (blk, blk), 0)
    q_pos = lax.broadcasted_iota(jnp.int32, (blk, blk), 1)
    causal = key_pos <= q_pos

    for slot, (q_ref, own) in enumerate(((qb_ref, blk_b), (qa_ref, blk_a))):
        for hh in heads:
            qaug_ref[slot, hh] = jnp.concatenate(
                [q_ref[hh * HEAD_DIM:(hh + 1) * HEAD_DIM, :], qtail_ref[own, hh]], axis=0)
            m_refs[hh][slot] = jnp.full((1, blk), NEG_INF, F32)
            acc_refs[hh][slot] = jnp.zeros((HEAD_DIM + SUM_ROWS, blk), F32)

    def unit(t):
        n = t - 2
        slot = jnp.where(t < 2, t, jnp.where(n >= blk_b, 1, 0))
        j = jnp.where(t == 0, blk_b, jnp.where(t == 1, blk_a, jnp.where(n >= blk_b, n - blk_b, n)))
        return slot, j

    def score(slot, j, buf, own):
        start = pl.multiple_of(j * blk, blk)
        for hh in heads:
            s = jnp.dot(k_ref[hh, pl.ds(start, blk), :], qaug_ref[slot, hh], preferred_element_type=F32)
            if own:
                s = jnp.where(causal, s, NEG_INF)
            s_ref[buf, hh] = s
            cmax_ref[buf, hh] = jnp.max(s, axis=0, keepdims=True)

    def accumulate(slot, j, buf):
        for hh in heads:
            m_old = m_refs[hh][slot]
            m_new = jnp.maximum(m_old, cmax_ref[buf, hh])
            e = jnp.exp2(s_ref[buf, hh] - m_new).astype(BF16)
            vt = jnp.concatenate([vt_ref[j, hh * HEAD_DIM:(hh + 1) * HEAD_DIM, :], ones_rows], axis=0)
            pv = jnp.dot(vt, e, preferred_element_type=F32)
            m_refs[hh][slot] = m_new
            acc_refs[hh][slot] = jnp.exp2(m_old - m_new) * acc_refs[hh][slot] + pv

    n_units = n_blk + 1
    score(*unit(0), 0, True)
    score(*unit(1), 1, True)

    def stages(t, buf):
        accumulate(*unit(t), buf)
        score(*unit(t + 2), buf, False)

    n_loop = (n_units - 2) // ATTN_UNROLL * ATTN_UNROLL

    def body(it, carry):
        for k in range(ATTN_UNROLL):
            stages(it * ATTN_UNROLL + k, k % 2)
        return carry

    lax.fori_loop(0, n_loop // ATTN_UNROLL, body, 0)
    for t in range(n_loop, n_units - 2):
        stages(t, t % 2)
    for t in range(n_units - 2, n_units):
        accumulate(*unit(t), t % 2)

    for slot in range(2):
        outs = []
        for hh in heads:
            acc = acc_refs[hh][slot]
            outs.append(acc[:HEAD_DIM] * (1.0 / acc[HEAD_DIM:HEAD_DIM + 1]))
        o_ref[1 - slot] = jnp.concatenate(outs, axis=0).T.astype(BF16)


def _moba(qt, k, vt, km, *, batch, seq, aw):
    n_blk = seq // MOBA_BLOCK
    n_h = ATTN_HEADS_PER_STEP
    width = n_h * HEAD_DIM
    kern = functools.partial(_moba_kernel, n_blk=n_blk)
    return pl.pallas_call(
        kern,
        grid=(batch, aw // width, n_blk // 2),
        in_specs=[pl.BlockSpec((None, None, width, MOBA_BLOCK), lambda b, hg, p: (b, p, hg, 0)),
                  pl.BlockSpec((None, None, width, MOBA_BLOCK), lambda b, hg, p: (b, n_blk - 1 - p, hg, 0)),
                  pl.BlockSpec((None, n_blk, width, MOBA_BLOCK), lambda b, hg, p: (b, 0, hg, 0)),
                  pl.BlockSpec((None, n_blk, width), lambda b, hg, p: (b, 0, hg)),
                  pl.BlockSpec((None, n_h, seq, LANES), lambda b, hg, p: (b, hg, 0, 0)),
                  pl.BlockSpec((None, n_blk, width, MOBA_BLOCK), lambda b, hg, p: (b, 0, hg, 0))],
        out_specs=pl.BlockSpec((None, 2, None, MOBA_BLOCK, width), lambda b, hg, p: (b, 0, p, 0, hg)),
        out_shape=jax.ShapeDtypeStruct((batch, 2, n_blk // 2, MOBA_BLOCK, aw), BF16),
        scratch_shapes=[pltpu.VMEM((n_blk, n_h, LANES - HEAD_DIM, MOBA_BLOCK), BF16),
                        pltpu.VMEM((2, n_h, LANES, MOBA_BLOCK), BF16),
                        pltpu.VMEM((2, n_h, MOBA_BLOCK, MOBA_BLOCK), F32),
                        pltpu.VMEM((2, n_h, 1, MOBA_BLOCK), F32)]
        + [pltpu.VMEM((2, 1, MOBA_BLOCK), F32)] * n_h
        + [pltpu.VMEM((2, HEAD_DIM + SUM_ROWS, MOBA_BLOCK), F32)] * n_h,
        compiler_params=_cparams(3),
        name="moba",
    )(qt, qt, qt, km, k, vt)


def _mixout_kernel(h_ref, mod_ref, a_lo_ref, a_hi_ref, u_ref, vln_ref, ws_ref, bias_ref, ga_ref, gg_ref, wout_ref,
                   o_ref, *, aw, gw, chunk):
    tm = h_ref.shape[0]
    gt = mod_ref[0, 5:6, :]
    attn = jnp.concatenate([a_lo_ref[...], a_hi_ref[...]], axis=0).astype(F32)
    attn_n = _rmsnorm(attn, ga_ref[...]).astype(BF16)

    n_grp = gw // LANES
    row = lax.broadcasted_iota(jnp.int32, (HEADS_PER_SLAB * chunk, chunk), 0)
    col = lax.broadcasted_iota(jnp.int32, (HEADS_PER_SLAB * chunk, chunk), 1)
    keep = (row % chunk) >= col
    w_pairs = [jnp.where(keep, ws_ref[g], 0.0).astype(BF16) for g in range(n_grp)]
    lane = lax.broadcasted_iota(jnp.int32, (chunk, LANES), 1)
    first_head = lane < HEAD_DIM
    bias = bias_ref[...]

    rows = []
    for c in range(tm // chunk):
        vc = vln_ref[c * chunk:(c + 1) * chunk, :]
        mixed = []
        for g in range(n_grp):
            r = jnp.dot(w_pairs[g], vc[:, g * LANES:(g + 1) * LANES], preferred_element_type=F32)
            mixed.append(jnp.where(first_head, r[:chunk], r[chunk:]))
        mixed = jnp.concatenate(mixed, axis=1) + bias
        rows.append(u_ref[c * chunk:(c + 1) * chunk, :].astype(F32) * mixed)
    gm = jnp.concatenate(rows, axis=0)
    gmlp_n = _rmsnorm(gm, gg_ref[...]).astype(BF16)

    y = jnp.dot(attn_n, wout_ref[:aw, :], preferred_element_type=F32)
    y = y + jnp.dot(gmlp_n, wout_ref[aw:, :], preferred_element_type=F32)
    o_ref[...] = h_ref[...] + gt * y


def _mixout(h, mod, attn, u, vln, w_s, b_s, g_attn, g_gmlp, w_out, *, seq, tm):
    n_tok, D = h.shape
    aw, gw = g_attn.shape[0], g_gmlp.shape[0]
    n_gh, chunk = w_s.shape[0], w_s.shape[1]
    assert tm == 2 * MOBA_BLOCK
    st = seq // tm
    half = seq // MOBA_BLOCK // 2
    ws_pairs = w_s.reshape(n_gh // HEADS_PER_SLAB, HEADS_PER_SLAB * chunk, chunk)
    bias = jnp.repeat(b_s.T, gw // n_gh, axis=1)

    def attn_spec(which):
        def index(i):
            j = (i % st) * 2 + which
            return (i // st, j // half, jnp.where(j < half, j, 2 * half - 1 - j), 0, 0)
        return pl.BlockSpec((None, None, None, MOBA_BLOCK, aw), index)

    kern = functools.partial(_mixout_kernel, aw=aw, gw=gw, chunk=chunk)
    return pl.pallas_call(
        kern,
        grid=(n_tok // tm,),
        in_specs=[pl.BlockSpec((tm, D), lambda i: (i, 0)),
                  pl.BlockSpec((1, N_MOD, D), lambda i: (i // st, 0, 0)),
                  attn_spec(0),
                  attn_spec(1),
                  pl.BlockSpec((tm, gw), lambda i: (i, 0)),
                  pl.BlockSpec((tm, gw), lambda i: (i, 0)),
                  _resident(ws_pairs.shape),
                  _resident((chunk, gw)),
                  _resident((1, aw)),
                  _resident((1, gw)),
                  _resident((aw + gw, D))],
        out_specs=pl.BlockSpec((tm, D), lambda i: (i, 0)),
        out_shape=jax.ShapeDtypeStruct((n_tok, D), F32),
        compiler_params=_cparams(1),
        name="mixout",
    )(h, mod, attn, attn, u, vln, ws_pairs, bias, g_attn.reshape(1, aw), g_gmlp.reshape(1, gw),
      w_out.astype(BF16))


def kernel(x, c, w_ada, b_ada, norm_ffn1, w_ffn1_gu, w_ffn1_down, norm_mix, w_in, gmlp_ln_g, gmlp_ln_b,
           gmlp_w_s, gmlp_b_s, g_attn_out, g_gmlp_out, w_out, norm_ffn2, w_ffn2_gu, w_ffn2_down, norm_final):
    B, S, D = x.shape
    depth = w_ada.shape[0]
    aw, gw = g_attn_out.shape[1], g_gmlp_out.shape[1]
    n_blk = S // MOBA_BLOCK
    assert S % (2 * MOBA_BLOCK) == 0 and aw % (ATTN_HEADS_PER_STEP * HEAD_DIM) == 0 and gw % LANES == 0
    assert n_blk <= LANES - HEAD_DIM and n_blk % 16 == 0
    tm = 2 * MOBA_BLOCK
    tm_ffn = 1024 if S % 1024 == 0 else tm
    h = x.reshape(B * S, D)
    for l in range(depth):
        last = l == depth - 1
        mod = _ada(c, w_ada[l], b_ada[l])
        h = _ffn(h, mod, norm_ffn1[l], w_ffn1_gu[l], w_ffn1_down[l], norm_final,
                 seq=S, mod_base=0, final_norm=False, tm=tm_ffn)
        qt, k, vt, km, u, vln = _inproj(h, mod, norm_mix[l], w_in[l], gmlp_ln_g[l], gmlp_ln_b[l],
                                        batch=B, seq=S, aw=aw, gw=gw, tm=tm)
        attn = _moba(qt, k, vt, km, batch=B, seq=S, aw=aw)
        h = _mixout(h, mod, attn, u, vln, gmlp_w_s[l], gmlp_b_s[l], g_attn_out[l], g_gmlp_out[l], w_out[l],
                    seq=S, tm=tm)
        h = _ffn(h, mod, norm_ffn2[l], w_ffn2_gu[l], w_ffn2_down[l], norm_final,
                 seq=S, mod_base=6, final_norm=last, tm=tm_ffn)
    return h.reshape(B, S, D)
```

```python
import functools

import jax
import jax.numpy as jnp
import numpy as np
from jax import lax
from jax.experimental import pallas as pl
from jax.experimental.pallas import tpu as pltpu

HEAD_DIM = 64
MOBA_BLOCK = 256
MOBA_TOPK = 3
ROPE_THETA = 10000.0
NORM_EPS = 1e-6
N_MOD = 9
NEG_INF = -1e30
LANES = 128
HEADS_PER_SLAB = LANES // HEAD_DIM
ATTN_HEADS_PER_STEP = 4
ATTN_UNROLL = 14
SUM_ROWS = 16
KM_ROWS = 8
VMEM_LIMIT_BYTES = 56 * 2**20

F32 = jnp.float32
BF16 = jnp.bfloat16


def _cparams(n_axes):
    return pltpu.CompilerParams(dimension_semantics=("arbitrary",) * n_axes,
                                vmem_limit_bytes=VMEM_LIMIT_BYTES)


def _resident(shape):
    nd = len(shape)
    return pl.BlockSpec(shape, lambda *_: (0,) * nd, pipeline_mode=pl.Buffered(1))


def _rmsnorm(x, g):
    return x * lax.rsqrt(jnp.mean(x * x, axis=-1, keepdims=True) + NORM_EPS) * g


def _ada_kernel(c_ref, w_ref, b_ref, o_ref):
    ca = jax.nn.silu(c_ref[...])
    o_ref[...] = jnp.dot(ca, w_ref[...], precision=lax.Precision.HIGHEST,
                         preferred_element_type=F32) + b_ref[...]


def _ada(c, w_ada, b_ada):
    B, D = c.shape
    n_out = w_ada.shape[1]
    rows = -(-B // 8) * 8
    c_pad = jnp.pad(c, ((0, rows - B), (0, 0)))
    tn = n_out // 8 if (n_out // 8) % LANES == 0 else n_out
    mod = pl.pallas_call(
        _ada_kernel,
        grid=(n_out // tn,),
        in_specs=[pl.BlockSpec((rows, D), lambda j: (0, 0)),
                  pl.BlockSpec((D, tn), lambda j: (0, j)),
                  pl.BlockSpec((1, tn), lambda j: (0, j))],
        out_specs=pl.BlockSpec((rows, tn), lambda j: (0, j)),
        out_shape=jax.ShapeDtypeStruct((rows, n_out), F32),
        compiler_params=_cparams(1),
        name="ada",
    )(c_pad, w_ada, b_ada.reshape(1, n_out))
    return mod[:B].reshape(B, N_MOD, D)


def _swiglu_halfstep(x, mod_ref, nw_ref, wgu_ref, wd_ref, *, mod_base, d_ff, tf):
    sh = mod_ref[0, mod_base:mod_base + 1, :]
    sc = mod_ref[0, mod_base + 1:mod_base + 2, :]
    gt = mod_ref[0, mod_base + 2:mod_base + 3, :]
    y = (_rmsnorm(x, nw_ref[...]) * (1.0 + sc) + sh).astype(BF16)
    acc = jnp.zeros(x.shape, F32)
    for c in range(d_ff // tf):
        g = jnp.dot(y, wgu_ref[:, c * tf:(c + 1) * tf], preferred_element_type=F32)
        u = jnp.dot(y, wgu_ref[:, d_ff + c * tf:d_ff + (c + 1) * tf], preferred_element_type=F32)
        mid = (jax.nn.silu(g) * u).astype(BF16)
        acc = acc + jnp.dot(mid, wd_ref[c * tf:(c + 1) * tf, :], preferred_element_type=F32)
    return x + (0.5 * gt) * acc


def _ffn_tile(d_ff):
    return 256 if d_ff % 256 == 0 else d_ff


def _ffn_kernel(h_ref, mod_ref, nw_ref, wgu_ref, wd_ref, o_ref, *, mod_base, d_ff, tf):
    o_ref[...] = _swiglu_halfstep(h_ref[...], mod_ref, nw_ref, wgu_ref, wd_ref, mod_base=mod_base, d_ff=d_ff, tf=tf)


def _ffn(h, mod, norm_w, w_gu, w_down, *, seq, mod_base, tm):
    n_tok, D = h.shape
    d_ff = w_down.shape[0]
    kern = functools.partial(_ffn_kernel, mod_base=mod_base, d_ff=d_ff, tf=_ffn_tile(d_ff))
    return pl.pallas_call(
        kern,
        grid=(n_tok // tm,),
        in_specs=[pl.BlockSpec((tm, D), lambda i: (i, 0)),
                  pl.BlockSpec((1, N_MOD, D), lambda i: (i * tm // seq, 0, 0)),
                  _resident((1, D)),
                  _resident((D, 2 * d_ff)),
                  _resident((d_ff, D))],
        out_specs=pl.BlockSpec((tm, D), lambda i: (i, 0)),
        out_shape=jax.ShapeDtypeStruct((n_tok, D), F32),
        compiler_params=_cparams(1),
        name="ffn",
    )(h, mod, norm_w.reshape(1, D), w_gu.astype(BF16), w_down.astype(BF16))


def _rope(x, cos, sin_signed, first_half):
    partner = jnp.where(first_half, pltpu.roll(x, LANES - HEAD_DIM // 2, 1), pltpu.roll(x, HEAD_DIM // 2, 1))
    return x * cos + partner * sin_signed


def _inproj_kernel(h_ref, mod_ref, nw_ref, win_ref, cos_ref, sin_ref, lng_ref, lnb_ref,
                   qt_ref, k_ref, vt_ref, km_ref, u_ref, vln_ref, *, aw, gw, st):
    x = h_ref[...]
    sh = mod_ref[0, 3:4, :]
    sc = mod_ref[0, 4:5, :]
    y = (_rmsnorm(x, nw_ref[...]) * (1.0 + sc) + sh).astype(BF16)
    proj = jnp.dot(y, win_ref[...], preferred_element_type=F32)
    cos = cos_ref[...]
    sin_signed = sin_ref[...]
    tm = x.shape[0]
    bpt = tm // MOBA_BLOCK
    lane = lax.broadcasted_iota(jnp.int32, cos.shape, 1)
    first_half = (lane % HEAD_DIM) < (HEAD_DIM // 2)
    is_key_lane = lane < HEAD_DIM
    row = lax.broadcasted_iota(jnp.int32, cos.shape, 0)
    blk_of_row = (pl.program_id(0) % st) * bpt + row // MOBA_BLOCK
    blk_onehot = jnp.where(lane - HEAD_DIM == blk_of_row, 1.0, 0.0)
    scale = HEAD_DIM ** -0.5 * np.log2(np.e)
    km_ref[...] = jnp.zeros(km_ref.shape, F32)
    for s in range(aw // LANES):
        qs = _rope(proj[:, s * LANES:(s + 1) * LANES], cos, sin_signed, first_half) * scale
        ks = _rope(proj[:, aw + s * LANES:aw + (s + 1) * LANES], cos, sin_signed, first_half)
        k_ref[2 * s] = jnp.where(is_key_lane, ks, blk_onehot).astype(BF16)
        k_ref[2 * s + 1] = jnp.where(is_key_lane, pltpu.roll(ks, HEAD_DIM, 1), blk_onehot).astype(BF16)
        vs = proj[:, 2 * aw + s * LANES:2 * aw + (s + 1) * LANES]
        for sb in range(bpt):
            rows = slice(sb * MOBA_BLOCK, (sb + 1) * MOBA_BLOCK)
            km_ref[sb:sb + 1, s * LANES:(s + 1) * LANES] = jnp.mean(ks[rows], axis=0, keepdims=True)
            qt_ref[sb, s * LANES:(s + 1) * LANES, :] = qs[rows].T.astype(BF16)
            vt_ref[sb, s * LANES:(s + 1) * LANES, :] = vs[rows].T.astype(BF16)
    u_ref[...] = jax.nn.gelu(proj[:, 3 * aw:3 * aw + gw]).astype(BF16)
    gv = jax.nn.gelu(proj[:, 3 * aw + gw:3 * aw + 2 * gw])
    mu = jnp.mean(gv, axis=-1, keepdims=True)
    var = jnp.mean(jnp.square(gv - mu), axis=-1, keepdims=True)
    vln = (gv - mu) * lax.rsqrt(var + NORM_EPS) * lng_ref[...] + lnb_ref[...]
    vln_ref[...] = vln.astype(BF16)


def _rope_tables(seq):
    half = HEAD_DIM // 2
    pos = jnp.arange(seq, dtype=F32)
    inv_freq = ROPE_THETA ** (-jnp.arange(half, dtype=F32) / half)
    ang = pos[:, None] * inv_freq[None, :]
    cos, sin = jnp.cos(ang), jnp.sin(ang)
    return (jnp.tile(jnp.concatenate([cos, cos], axis=-1), (1, HEADS_PER_SLAB)),
            jnp.tile(jnp.concatenate([-sin, sin], axis=-1), (1, HEADS_PER_SLAB)))


def _inproj(h, mod, norm_w, w_in, ln_g, ln_b, *, batch, seq, aw, gw, tm):
    n_tok, D = h.shape
    n_heads = aw // HEAD_DIM
    st = seq // tm
    bpt = tm // MOBA_BLOCK
    n_blk = seq // MOBA_BLOCK
    cos_t, sin_t = _rope_tables(seq)
    kern = functools.partial(_inproj_kernel, aw=aw, gw=gw, st=st)
    qt, k, vt, km, u, vln = pl.pallas_call(
        kern,
        grid=(n_tok // tm,),
        in_specs=[pl.BlockSpec((tm, D), lambda i: (i, 0)),
                  pl.BlockSpec((1, N_MOD, D), lambda i: (i // st, 0, 0)),
                  _resident((1, D)),
                  _resident((D, 3 * aw + 2 * gw)),
                  pl.BlockSpec((tm, LANES), lambda i: (i % st, 0)),
                  pl.BlockSpec((tm, LANES), lambda i: (i % st, 0)),
                  _resident((1, gw)),
                  _resident((1, gw))],
        out_specs=[pl.BlockSpec((None, bpt, aw, MOBA_BLOCK), lambda i: (i // st, i % st, 0, 0)),
                   pl.BlockSpec((None, n_heads, tm, LANES), lambda i: (i // st, 0, i % st, 0)),
                   pl.BlockSpec((None, bpt, aw, MOBA_BLOCK), lambda i: (i // st, i % st, 0, 0)),
                   pl.BlockSpec((None, None, KM_ROWS, aw), lambda i: (i // st, i % st, 0, 0)),
                   pl.BlockSpec((tm, gw), lambda i: (i, 0)),
                   pl.BlockSpec((tm, gw), lambda i: (i, 0))],
        out_shape=[jax.ShapeDtypeStruct((batch, n_blk, aw, MOBA_BLOCK), BF16),
                   jax.ShapeDtypeStruct((batch, n_heads, seq, LANES), BF16),
                   jax.ShapeDtypeStruct((batch, n_blk, aw, MOBA_BLOCK), BF16),
                   jax.ShapeDtypeStruct((batch, st, KM_ROWS, aw), F32),
                   jax.ShapeDtypeStruct((n_tok, gw), BF16),
                   jax.ShapeDtypeStruct((n_tok, gw), BF16)],
        compiler_params=_cparams(1),
        name="inproj",
    )(h, mod, norm_w.reshape(1, D), w_in.astype(BF16), cos_t, sin_t, ln_g.reshape(1, gw), ln_b.reshape(1, gw))
    km = km[:, :, :bpt, :].reshape(batch, n_blk, aw)
    return qt, k, vt, km, u, vln


def _moba_kernel(qa_ref, qb_ref, qt_all_ref, km_ref, k_ref, vt_ref, o_ref, qtail_ref, qaug_ref, s_ref, cmax_ref,
                 *state_refs, n_blk):
    p = pl.program_id(2)
    blk = MOBA_BLOCK
    n_h = ATTN_HEADS_PER_STEP
    heads = range(n_h)
    m_refs, acc_refs = state_refs[:n_h], state_refs[n_h:]
    zeros_half = jnp.zeros((HEAD_DIM, blk), BF16)

    @pl.when(p == 0)
    def _():
        blk_id = lax.broadcasted_iota(jnp.int32, (n_blk, blk), 0)
        bias_pad = jnp.zeros((LANES - HEAD_DIM - n_blk, blk), BF16)
        km_pieces = []
        for slab in range(n_h // HEADS_PER_SLAB):
            rest = km_ref[:, slab * LANES:(slab + 1) * LANES]
            pieces = []
            for _ in range(3):
                piece = rest.astype(BF16)
                rest = rest - piece.astype(F32)
                pieces.append(piece)
            km_pieces.append(pieces)

        def gate_body(ib, carry):
            for hh in heads:
                qt = qt_all_ref[ib, hh * HEAD_DIM:(hh + 1) * HEAD_DIM, :]
                halves = [zeros_half, qt] if hh % HEADS_PER_SLAB else [qt, zeros_half]
                q_slab = jnp.concatenate(halves, axis=0)
                gate = sum(jnp.dot(piece, q_slab, preferred_element_type=F32)
                           for piece in reversed(km_pieces[hh // HEADS_PER_SLAB]))
                gate = jnp.where(blk_id < ib, gate, NEG_INF)
                bias = jnp.where(blk_id == ib, 0.0, NEG_INF)
                for r in range(min(MOBA_TOPK, n_blk)):
                    top = jnp.max(gate, axis=0, keepdims=True)
                    idx = jnp.min(jnp.where(gate == top, blk_id, n_blk), axis=0, keepdims=True)
                    hit = blk_id == idx
                    rank_bias = jnp.where(r < ib, 0.0, NEG_INF)
                    bias = jnp.where(hit, jnp.maximum(bias, rank_bias), bias)
                    gate = jnp.where(hit, -jnp.inf, gate)
                qtail_ref[ib, hh] = jnp.concatenate([bias.astype(BF16), bias_pad], axis=0)
            return carry

        lax.fori_loop(0, n_blk, gate_body, 0)

    blk_a, blk_b = p, n_blk - 1 - p
    ones_rows = jnp.ones((SUM_ROWS, blk), BF16)
    key_pos = lax.broadcasted_iota(jnp.int32, (blk, blk), 0)
    q_pos = lax.broadcasted_iota(jnp.int32, (blk, blk), 1)
    causal = key_pos <= q_pos

    for slot, (q_ref, own) in enumerate(((qb_ref, blk_b), (qa_ref, blk_a))):
        for hh in heads:
            qaug_ref[slot, hh] = jnp.concatenate(
                [q_ref[hh * HEAD_DIM:(hh + 1) * HEAD_DIM, :], qtail_ref[own, hh]], axis=0)
            m_refs[hh][slot] = jnp.full((1, blk), NEG_INF, F32)
            acc_refs[hh][slot] = jnp.zeros((HEAD_DIM + SUM_ROWS, blk), F32)

    def unit(t):
        n = t - 2
        slot = jnp.where(t < 2, t, jnp.where(n >= blk_b, 1, 0))
        j = jnp.where(t == 0, blk_b, jnp.where(t == 1, blk_a, jnp.where(n >= blk_b, n - blk_b, n)))
        return slot, j

    def score(slot, j, buf, own):
        start = pl.multiple_of(j * blk, blk)
        for hh in heads:
            s = jnp.dot(k_ref[hh, pl.ds(start, blk), :], qaug_ref[slot, hh], preferred_element_type=F32)
            if own:
                s = jnp.where(causal, s, NEG_INF)
            s_ref[buf, hh] = s
            cmax_ref[buf, hh] = jnp.max(s, axis=0, keepdims=True)

    def accumulate(slot, j, buf):
        for hh in heads:
            m_old = m_refs[hh][slot]
            m_new = jnp.maximum(m_old, cmax_ref[buf, hh])
            e = jnp.exp2(s_ref[buf, hh] - m_new).astype(BF16)
            vt = jnp.concatenate([vt_ref[j, hh * HEAD_DIM:(hh + 1) * HEAD_DIM, :], ones_rows], axis=0)
            pv = jnp.dot(vt, e, preferred_element_type=F32)
            m_refs[hh][slot] = m_new
            acc_refs[hh][slot] = jnp.exp2(m_old - m_new) * acc_refs[hh][slot] + pv

    n_units = n_blk + 1
    score(*unit(0), 0, True)
    score(*unit(1), 1, True)

    def stages(t, buf):
        accumulate(*unit(t), buf)
        score(*unit(t + 2), buf, False)

    n_loop = (n_units - 2) // ATTN_UNROLL * ATTN_UNROLL

    def body(it, carry):
        for k in range(ATTN_UNROLL):
            stages(it * ATTN_UNROLL + k, k % 2)
        return carry

    lax.fori_loop(0, n_loop // ATTN_UNROLL, body, 0)
    for t in range(n_loop, n_units - 2):
        stages(t, t % 2)
    for t in range(n_units - 2, n_units):
        accumulate(*unit(t), t % 2)

    for slot in range(2):
        outs = []
        for hh in heads:
            acc = acc_refs[hh][slot]
            outs.append(acc[:HEAD_DIM] * (1.0 / acc[HEAD_DIM:HEAD_DIM + 1]))
        o_ref[1 - slot] = jnp.concatenate(outs, axis=0).T.astype(BF16)


def _moba(qt, k, vt, km, *, batch, seq, aw):
    n_blk = seq // MOBA_BLOCK
    n_h = ATTN_HEADS_PER_STEP
    width = n_h * HEAD_DIM
    kern = functools.partial(_moba_kernel, n_blk=n_blk)
    return pl.pallas_call(
        kern,
        grid=(batch, aw // width, n_blk // 2),
        in_specs=[pl.BlockSpec((None, None, width, MOBA_BLOCK), lambda b, hg, p: (b, p, hg, 0)),
                  pl.BlockSpec((None, None, width, MOBA_BLOCK), lambda b, hg, p: (b, n_blk - 1 - p, hg, 0)),
                  pl.BlockSpec((None, n_blk, width, MOBA_BLOCK), lambda b, hg, p: (b, 0, hg, 0)),
                  pl.BlockSpec((None, n_blk, width), lambda b, hg, p: (b, 0, hg)),
                  pl.BlockSpec((None, n_h, seq, LANES), lambda b, hg, p: (b, hg, 0, 0)),
                  pl.BlockSpec((None, n_blk, width, MOBA_BLOCK), lambda b, hg, p: (b, 0, hg, 0))],
        out_specs=pl.BlockSpec((None, 2, None, MOBA_BLOCK, width), lambda b, hg, p: (b, 0, p, 0, hg)),
        out_shape=jax.ShapeDtypeStruct((batch, 2, n_blk // 2, MOBA_BLOCK, aw), BF16),
        scratch_shapes=[pltpu.VMEM((n_blk, n_h, LANES - HEAD_DIM, MOBA_BLOCK), BF16),
                        pltpu.VMEM((2, n_h, LANES, MOBA_BLOCK), BF16),
                        pltpu.VMEM((2, n_h, MOBA_BLOCK, MOBA_BLOCK), F32),
                        pltpu.VMEM((2, n_h, 1, MOBA_BLOCK), F32)]
        + [pltpu.VMEM((2, 1, MOBA_BLOCK), F32)] * n_h
        + [pltpu.VMEM((2, HEAD_DIM + SUM_ROWS, MOBA_BLOCK), F32)] * n_h,
        compiler_params=_cparams(3),
        name="moba",
    )(qt, qt, qt, km, k, vt)


def _mixffn_kernel(h_ref, mod_ref, a_lo_ref, a_hi_ref, u_ref, vln_ref, ws_ref, bias_ref, ga_ref, gg_ref, wout_ref,
                   nw_ref, wgu_ref, wd_ref, nf_ref, o_ref, *, aw, gw, chunk, d_ff, tf, final_norm):
    tm = h_ref.shape[0]
    gt = mod_ref[0, 5:6, :]
    attn = jnp.concatenate([a_lo_ref[...], a_hi_ref[...]], axis=0).astype(F32)
    attn_n = _rmsnorm(attn, ga_ref[...]).astype(BF16)

    n_grp = gw // LANES
    row = lax.broadcasted_iota(jnp.int32, (HEADS_PER_SLAB * chunk, chunk), 0)
    col = lax.broadcasted_iota(jnp.int32, (HEADS_PER_SLAB * chunk, chunk), 1)
    keep = (row % chunk) >= col
    w_pairs = [jnp.where(keep, ws_ref[g], 0.0).astype(BF16) for g in range(n_grp)]
    lane = lax.broadcasted_iota(jnp.int32, (chunk, LANES), 1)
    first_head = lane < HEAD_DIM
    bias = bias_ref[...]

    rows = []
    for c in range(tm // chunk):
        vc = vln_ref[c * chunk:(c + 1) * chunk, :]
        mixed = []
        for g in range(n_grp):
            r = jnp.dot(w_pairs[g], vc[:, g * LANES:(g + 1) * LANES], preferred_element_type=F32)
            mixed.append(jnp.where(first_head, r[:chunk], r[chunk:]))
        mixed = jnp.concatenate(mixed, axis=1) + bias
        rows.append(u_ref[c * chunk:(c + 1) * chunk, :].astype(F32) * mixed)
    gm = jnp.concatenate(rows, axis=0)
    gmlp_n = _rmsnorm(gm, gg_ref[...]).astype(BF16)

    y = jnp.dot(attn_n, wout_ref[:aw, :], preferred_element_type=F32)
    y = y + jnp.dot(gmlp_n, wout_ref[aw:, :], preferred_element_type=F32)
    h2 = h_ref[...] + gt * y
    out = _swiglu_halfstep(h2, mod_ref, nw_ref, wgu_ref, wd_ref, mod_base=6, d_ff=d_ff, tf=tf)
    if final_norm:
        out = _rmsnorm(out, nf_ref[...])
    o_ref[...] = out


def _mixffn(h, mod, attn, u, vln, w_s, b_s, g_attn, g_gmlp, w_out, norm_w, w_gu, w_down, norm_final,
            *, seq, tm, final_norm):
    n_tok, D = h.shape
    aw, gw = g_attn.shape[0], g_gmlp.shape[0]
    n_gh, chunk = w_s.shape[0], w_s.shape[1]
    d_ff = w_down.shape[0]
    assert tm == 2 * MOBA_BLOCK
    st = seq // tm
    half = seq // MOBA_BLOCK // 2
    ws_pairs = w_s.reshape(n_gh // HEADS_PER_SLAB, HEADS_PER_SLAB * chunk, chunk)
    bias = jnp.repeat(b_s.T, gw // n_gh, axis=1)

    def attn_spec(which):
        def index(i):
            j = (i % st) * 2 + which
            return (i // st, j // half, jnp.where(j < half, j, 2 * half - 1 - j), 0, 0)
        return pl.BlockSpec((None, None, None, MOBA_BLOCK, aw), index)

    kern = functools.partial(_mixffn_kernel, aw=aw, gw=gw, chunk=chunk, d_ff=d_ff, tf=_ffn_tile(d_ff),
                             final_norm=final_norm)
    return pl.pallas_call(
        kern,
        grid=(n_tok // tm,),
        in_specs=[pl.BlockSpec((tm, D), lambda i: (i, 0)),
                  pl.BlockSpec((1, N_MOD, D), lambda i: (i // st, 0, 0)),
                  attn_spec(0),
                  attn_spec(1),
                  pl.BlockSpec((tm, gw), lambda i: (i, 0)),
                  pl.BlockSpec((tm, gw), lambda i: (i, 0)),
                  _resident(ws_pairs.shape),
                  _resident((chunk, gw)),
                  _resident((1, aw)),
                  _resident((1, gw)),
                  _resident((aw + gw, D)),
                  _resident((1, D)),
                  _resident((D, 2 * d_ff)),
                  _resident((d_ff, D)),
                  _resident((1, D))],
        out_specs=pl.BlockSpec((tm, D), lambda i: (i, 0)),
        out_shape=jax.ShapeDtypeStruct((n_tok, D), F32),
        compiler_params=_cparams(1),
        name="mixffn",
    )(h, mod, attn, attn, u, vln, ws_pairs, bias, g_attn.reshape(1, aw), g_gmlp.reshape(1, gw),
      w_out.astype(BF16), norm_w.reshape(1, D), w_gu.astype(BF16), w_down.astype(BF16), norm_final.reshape(1, D))


def kernel(x, c, w_ada, b_ada, norm_ffn1, w_ffn1_gu, w_ffn1_down, norm_mix, w_in, gmlp_ln_g, gmlp_ln_b,
           gmlp_w_s, gmlp_b_s, g_attn_out, g_gmlp_out, w_out, norm_ffn2, w_ffn2_gu, w_ffn2_down, norm_final):
    B, S, D = x.shape
    depth = w_ada.shape[0]
    aw, gw = g_attn_out.shape[1], g_gmlp_out.shape[1]
    n_blk = S // MOBA_BLOCK
    assert S % (2 * MOBA_BLOCK) == 0 and aw % (ATTN_HEADS_PER_STEP * HEAD_DIM) == 0 and gw % LANES == 0
    assert n_blk <= LANES - HEAD_DIM and n_blk % 16 == 0
    tm = 2 * MOBA_BLOCK
    tm_ffn = 1024 if S % 1024 == 0 else tm
    h = x.reshape(B * S, D)
    for l in range(depth):
        last = l == depth - 1
        mod = _ada(c, w_ada[l], b_ada[l])
        h = _ffn(h, mod, norm_ffn1[l], w_ffn1_gu[l], w_ffn1_down[l], seq=S, mod_base=0, tm=tm_ffn)
        qt, k, vt, km, u, vln = _inproj(h, mod, norm_mix[l], w_in[l], gmlp_ln_g[l], gmlp_ln_b[l],
                                        batch=B, seq=S, aw=aw, gw=gw, tm=tm)
        attn = _moba(qt, k, vt, km, batch=B, seq=S, aw=aw)
        h = _mixffn(h, mod, attn, u, vln, gmlp_w_s[l], gmlp_b_s[l], g_attn_out[l], g_gmlp_out[l], w_out[l],
                    norm_ffn2[l], w_ffn2_gu[l], w_ffn2_down[l], norm_final, seq=S, tm=tm, final_norm=last)
    return h.reshape(B, S, D)
```

```python
import functools

import jax
import jax.numpy as jnp
import numpy as np
from jax import lax
from jax.experimental import pallas as pl
from jax.experimental.pallas import tpu as pltpu

HEAD_DIM = 64
MOBA_BLOCK = 256
MOBA_TOPK = 3
ROPE_THETA = 10000.0
NORM_EPS = 1e-6
N_MOD = 9
NEG_INF = -1e30
LANES = 128
HEADS_PER_SLAB = LANES // HEAD_DIM
ATTN_HEADS_PER_STEP = 4
ATTN_UNROLL = 14
SUM_ROWS = 16
KM_ROWS = 8
VMEM_LIMIT_BYTES = 56 * 2**20

F32 = jnp.float32
BF16 = jnp.bfloat16


def _cparams(n_axes):
    return pltpu.CompilerParams(dimension_semantics=("arbitrary",) * n_axes,
                                vmem_limit_bytes=VMEM_LIMIT_BYTES)


def _resident(shape):
    nd = len(shape)
    return pl.BlockSpec(shape, lambda *_: (0,) * nd, pipeline_mode=pl.Buffered(1))


def _rmsnorm(x, g):
    return x * lax.rsqrt(jnp.mean(x * x, axis=-1, keepdims=True) + NORM_EPS) * g


def _ada_kernel(c_ref, w_ref, b_ref, o_ref):
    ca = jax.nn.silu(c_ref[...])
    o_ref[...] = jnp.dot(ca, w_ref[...], precision=lax.Precision.HIGHEST,
                         preferred_element_type=F32) + b_ref[...]


def _ada(c, w_ada, b_ada):
    B, D = c.shape
    n_out = w_ada.shape[1]
    rows = -(-B // 8) * 8
    c_pad = jnp.pad(c, ((0, rows - B), (0, 0)))
    tn = n_out // 8 if (n_out // 8) % LANES == 0 else n_out
    mod = pl.pallas_call(
        _ada_kernel,
        grid=(n_out // tn,),
        in_specs=[pl.BlockSpec((rows, D), lambda j: (0, 0)),
                  pl.BlockSpec((D, tn), lambda j: (0, j)),
                  pl.BlockSpec((1, tn), lambda j: (0, j))],
        out_specs=pl.BlockSpec((rows, tn), lambda j: (0, j)),
        out_shape=jax.ShapeDtypeStruct((rows, n_out), F32),
        compiler_params=_cparams(1),
        name="ada",
    )(c_pad, w_ada, b_ada.reshape(1, n_out))
    return mod[:B].reshape(B, N_MOD, D)


def _swiglu_halfstep(x, mod_ref, nw_ref, wgu_ref, wd_ref, *, mod_base, d_ff, tf):
    sh = mod_ref[0, mod_base:mod_base + 1, :]
    sc = mod_ref[0, mod_base + 1:mod_base + 2, :]
    gt = mod_ref[0, mod_base + 2:mod_base + 3, :]
    y = (_rmsnorm(x, nw_ref[...]) * (1.0 + sc) + sh).astype(BF16)
    acc = jnp.zeros(x.shape, F32)
    for c in range(d_ff // tf):
        g = jnp.dot(y, wgu_ref[:, c * tf:(c + 1) * tf], preferred_element_type=F32)
        u = jnp.dot(y, wgu_ref[:, d_ff + c * tf:d_ff + (c + 1) * tf], preferred_element_type=F32)
        mid = (jax.nn.silu(g) * u).astype(BF16)
        acc = acc + jnp.dot(mid, wd_ref[c * tf:(c + 1) * tf, :], preferred_element_type=F32)
    return x + (0.5 * gt) * acc


def _ffn_tile(d_ff):
    return 256 if d_ff % 256 == 0 else d_ff


def _ffn_kernel(h_ref, mod_ref, nw_ref, wgu_ref, wd_ref, *rest, mod_base, d_ff, tf, n_casts):
    cast_in, o_ref, cast_out = rest[:n_casts], rest[n_casts], rest[n_casts + 1:]
    o_ref[...] = _swiglu_halfstep(h_ref[...], mod_ref, nw_ref, wgu_ref, wd_ref, mod_base=mod_base, d_ff=d_ff, tf=tf)
    for src, dst in zip(cast_in, cast_out):
        dst[...] = src[...].astype(BF16)


def _as_row_chunks(w, steps):
    rows, cols = w.shape
    while rows % (16 * steps) and cols % (2 * LANES) == 0:
        rows, cols = rows * 2, cols // 2
    assert rows % (16 * steps) == 0 and cols % LANES == 0, (w.shape, steps)
    return w.reshape(rows, cols)


def _ffn(h, mod, norm_w, w_gu, w_down, later_weights, *, seq, mod_base, tm):
    n_tok, D = h.shape
    d_ff = w_down.shape[0]
    steps = n_tok // tm
    chunked = [_as_row_chunks(w, steps) for w in later_weights]
    chunk_specs = [pl.BlockSpec((w.shape[0] // steps, w.shape[1]), lambda i: (i, 0)) for w in chunked]
    kern = functools.partial(_ffn_kernel, mod_base=mod_base, d_ff=d_ff, tf=_ffn_tile(d_ff), n_casts=len(chunked))
    out, *casts = pl.pallas_call(
        kern,
        grid=(steps,),
        in_specs=[pl.BlockSpec((tm, D), lambda i: (i, 0)),
                  pl.BlockSpec((1, N_MOD, D), lambda i: (i * tm // seq, 0, 0)),
                  _resident((1, D)),
                  _resident((D, 2 * d_ff)),
                  _resident((d_ff, D))] + chunk_specs,
        out_specs=[pl.BlockSpec((tm, D), lambda i: (i, 0))] + chunk_specs,
        out_shape=[jax.ShapeDtypeStruct((n_tok, D), F32)]
        + [jax.ShapeDtypeStruct(w.shape, BF16) for w in chunked],
        compiler_params=_cparams(1),
        name="ffn",
    )(h, mod, norm_w.reshape(1, D), w_gu.astype(BF16), w_down.astype(BF16), *chunked)
    return out, [c.reshape(w.shape) for c, w in zip(casts, later_weights)]


def _rope(x, cos, sin_signed, first_half):
    partner = jnp.where(first_half, pltpu.roll(x, LANES - HEAD_DIM // 2, 1), pltpu.roll(x, HEAD_DIM // 2, 1))
    return x * cos + partner * sin_signed


def _inproj_kernel(h_ref, mod_ref, nw_ref, win_ref, cos_ref, sin_ref, lng_ref, lnb_ref,
                   qt_ref, k_ref, vt_ref, km_ref, u_ref, vln_ref, *, aw, gw, st):
    x = h_ref[...]
    sh = mod_ref[0, 3:4, :]
    sc = mod_ref[0, 4:5, :]
    y = (_rmsnorm(x, nw_ref[...]) * (1.0 + sc) + sh).astype(BF16)

    def project(col, width):
        return jnp.dot(y, win_ref[:, col:col + width], preferred_element_type=F32)

    cos = cos_ref[...]
    sin_signed = sin_ref[...]
    tm = x.shape[0]
    bpt = tm // MOBA_BLOCK
    lane = lax.broadcasted_iota(jnp.int32, cos.shape, 1)
    first_half = (lane % HEAD_DIM) < (HEAD_DIM // 2)
    is_key_lane = lane < HEAD_DIM
    row = lax.broadcasted_iota(jnp.int32, cos.shape, 0)
    blk_of_row = (pl.program_id(0) % st) * bpt + row // MOBA_BLOCK
    blk_onehot = jnp.where(lane - HEAD_DIM == blk_of_row, 1.0, 0.0)
    scale = HEAD_DIM ** -0.5 * np.log2(np.e)
    km_ref[...] = jnp.zeros(km_ref.shape, F32)
    blocks = [slice(sb * MOBA_BLOCK, (sb + 1) * MOBA_BLOCK) for sb in range(bpt)]
    gv = jax.nn.gelu(project(3 * aw + gw, gw))
    mu = jnp.mean(gv, axis=-1, keepdims=True)
    var = jnp.mean(jnp.square(gv - mu), axis=-1, keepdims=True)
    vln = (gv - mu) * lax.rsqrt(var + NORM_EPS) * lng_ref[...] + lnb_ref[...]
    vln_ref[...] = vln.astype(BF16)
    q = project(0, aw)
    for s in range(aw // LANES):
        qs = _rope(q[:, s * LANES:(s + 1) * LANES], cos, sin_signed, first_half) * scale
        for sb, rows in enumerate(blocks):
            qt_ref[sb, s * LANES:(s + 1) * LANES, :] = qs[rows].T.astype(BF16)
    k = project(aw, aw)
    for s in range(aw // LANES):
        ks = _rope(k[:, s * LANES:(s + 1) * LANES], cos, sin_signed, first_half)
        k_ref[2 * s] = jnp.where(is_key_lane, ks, blk_onehot).astype(BF16)
        k_ref[2 * s + 1] = jnp.where(is_key_lane, pltpu.roll(ks, HEAD_DIM, 1), blk_onehot).astype(BF16)
        for sb, rows in enumerate(blocks):
            km_ref[sb:sb + 1, s * LANES:(s + 1) * LANES] = jnp.mean(ks[rows], axis=0, keepdims=True)
    u_ref[...] = jax.nn.gelu(project(3 * aw, gw)).astype(BF16)
    v = project(2 * aw, aw)
    for s in range(aw // LANES):
        for sb, rows in enumerate(blocks):
            vt_ref[sb, s * LANES:(s + 1) * LANES, :] = v[rows, s * LANES:(s + 1) * LANES].T.astype(BF16)


def _rope_tables(seq):
    half = HEAD_DIM // 2
    pos = jnp.arange(seq, dtype=F32)
    inv_freq = ROPE_THETA ** (-jnp.arange(half, dtype=F32) / half)
    ang = pos[:, None] * inv_freq[None, :]
    cos, sin = jnp.cos(ang), jnp.sin(ang)
    return (jnp.tile(jnp.concatenate([cos, cos], axis=-1), (1, HEADS_PER_SLAB)),
            jnp.tile(jnp.concatenate([-sin, sin], axis=-1), (1, HEADS_PER_SLAB)))


def _inproj(h, mod, norm_w, w_in, ln_g, ln_b, *, batch, seq, aw, gw, tm):
    n_tok, D = h.shape
    n_heads = aw // HEAD_DIM
    st = seq // tm
    bpt = tm // MOBA_BLOCK
    n_blk = seq // MOBA_BLOCK
    cos_t, sin_t = _rope_tables(seq)
    kern = functools.partial(_inproj_kernel, aw=aw, gw=gw, st=st)
    qt, k, vt, km, u, vln = pl.pallas_call(
        kern,
        grid=(n_tok // tm,),
        in_specs=[pl.BlockSpec((tm, D), lambda i: (i, 0)),
                  pl.BlockSpec((1, N_MOD, D), lambda i: (i // st, 0, 0)),
                  _resident((1, D)),
                  _resident((D, 3 * aw + 2 * gw)),
                  pl.BlockSpec((tm, LANES), lambda i: (i % st, 0)),
                  pl.BlockSpec((tm, LANES), lambda i: (i % st, 0)),
                  _resident((1, gw)),
                  _resident((1, gw))],
        out_specs=[pl.BlockSpec((None, bpt, aw, MOBA_BLOCK), lambda i: (i // st, i % st, 0, 0)),
                   pl.BlockSpec((None, n_heads, tm, LANES), lambda i: (i // st, 0, i % st, 0)),
                   pl.BlockSpec((None, bpt, aw, MOBA_BLOCK), lambda i: (i // st, i % st, 0, 0)),
                   pl.BlockSpec((None, None, KM_ROWS, aw), lambda i: (i // st, i % st, 0, 0)),
                   pl.BlockSpec((tm, gw), lambda i: (i, 0)),
                   pl.BlockSpec((tm, gw), lambda i: (i, 0))],
        out_shape=[jax.ShapeDtypeStruct((batch, n_blk, aw, MOBA_BLOCK), BF16),
                   jax.ShapeDtypeStruct((batch, n_heads, seq, LANES), BF16),
                   jax.ShapeDtypeStruct((batch, n_blk, aw, MOBA_BLOCK), BF16),
                   jax.ShapeDtypeStruct((batch, st, KM_ROWS, aw), F32),
                   jax.ShapeDtypeStruct((n_tok, gw), BF16),
                   jax.ShapeDtypeStruct((n_tok, gw), BF16)],
        compiler_params=_cparams(1),
        name="inproj",
    )(h, mod, norm_w.reshape(1, D), w_in.astype(BF16), cos_t, sin_t, ln_g.reshape(1, gw), ln_b.reshape(1, gw))
    km = km[:, :, :bpt, :].reshape(batch, n_blk, aw)
    return qt, k, vt, km, u, vln


def _moba_kernel(qa_ref, qb_ref, qt_all_ref, km_ref, k_ref, vt_ref, o_ref, qtail_ref, qaug_ref, s_ref, cmax_ref,
                 *state_refs, n_blk):
    p = pl.program_id(2)
    blk = MOBA_BLOCK
    n_h = ATTN_HEADS_PER_STEP
    heads = range(n_h)
    m_refs, acc_refs = state_refs[:n_h], state_refs[n_h:]
    zeros_half = jnp.zeros((HEAD_DIM, blk), BF16)

    @pl.when(p == 0)
    def _():
        blk_id = lax.broadcasted_iota(jnp.int32, (n_blk, blk), 0)
        bias_pad = jnp.zeros((LANES - HEAD_DIM - n_blk, blk), BF16)
        km_pieces = []
        for slab in range(n_h // HEADS_PER_SLAB):
            rest = km_ref[:, slab * LANES:(slab + 1) * LANES]
            pieces = []
            for _ in range(3):
                piece = rest.astype(BF16)
                rest = rest - piece.astype(F32)
                pieces.append(piece)
            km_pieces.append(pieces)

        def gate_body(ib, carry):
            for hh in heads:
                qt = qt_all_ref[ib, hh * HEAD_DIM:(hh + 1) * HEAD_DIM, :]
                halves = [zeros_half, qt] if hh % HEADS_PER_SLAB else [qt, zeros_half]
                q_slab = jnp.concatenate(halves, axis=0)
                gate = sum(jnp.dot(piece, q_slab, preferred_element_type=F32)
                           for piece in reversed(km_pieces[hh // HEADS_PER_SLAB]))
                gate = jnp.where(blk_id < ib, gate, NEG_INF)
                bias = jnp.where(blk_id == ib, 0.0, NEG_INF)
                for r in range(min(MOBA_TOPK, n_blk)):
                    top = jnp.max(gate, axis=0, keepdims=True)
                    idx = jnp.min(jnp.where(gate == top, blk_id, n_blk), axis=0, keepdims=True)
                    hit = blk_id == idx
                    rank_bias = jnp.where(r < ib, 0.0, NEG_INF)
                    bias = jnp.where(hit, jnp.maximum(bias, rank_bias), bias)
                    gate = jnp.where(hit, -jnp.inf, gate)
                qtail_ref[ib, hh] = jnp.concatenate([bias.astype(BF16), bias_pad], axis=0)
            return carry

        lax.fori_loop(0, n_blk, gate_body, 0, unroll=2)

    blk_a, blk_b = p, n_blk - 1 - p
    ones_rows = jnp.ones((SUM_ROWS, blk), BF16)
    key_pos = lax.broadcasted_iota(jnp.int32, (blk, blk), 0)
    q_pos = lax.broadcasted_iota(jnp.int32, (blk, blk), 1)
    causal = key_pos <= q_pos

    for slot, (q_ref, own) in enumerate(((qb_ref, blk_b), (qa_ref, blk_a))):
        for hh in heads:
            qaug_ref[slot, hh] = jnp.concatenate(
                [q_ref[hh * HEAD_DIM:(hh + 1) * HEAD_DIM, :], qtail_ref[own, hh]], axis=0)
            m_refs[hh][slot] = jnp.full((1, blk), NEG_INF, F32)
            acc_refs[hh][slot] = jnp.zeros((HEAD_DIM + SUM_ROWS, blk), F32)

    def unit(t):
        n = t - 2
        slot = jnp.where(t < 2, t, jnp.where(n >= blk_b, 1, 0))
        j = jnp.where(t == 0, blk_b, jnp.where(t == 1, blk_a, jnp.where(n >= blk_b, n - blk_b, n)))
        return slot, j

    def score(slot, j, buf, own):
        start = pl.multiple_of(j * blk, blk)
        for hh in heads:
            s = jnp.dot(k_ref[hh, pl.ds(start, blk), :], qaug_ref[slot, hh], preferred_element_type=F32)
            if own:
                s = jnp.where(causal, s, NEG_INF)
            s_ref[buf, hh] = s
            cmax_ref[buf, hh] = jnp.max(s, axis=0, keepdims=True)

    def accumulate(slot, j, buf):
        for hh in heads:
            m_old = m_refs[hh][slot]
            m_new = jnp.maximum(m_old, cmax_ref[buf, hh])
            e = jnp.exp2(s_ref[buf, hh] - m_new).astype(BF16)
            vt = jnp.concatenate([vt_ref[j, hh * HEAD_DIM:(hh + 1) * HEAD_DIM, :], ones_rows], axis=0)
            pv = jnp.dot(vt, e, preferred_element_type=F32)
            m_refs[hh][slot] = m_new
            acc_refs[hh][slot] = jnp.exp2(m_old - m_new) * acc_refs[hh][slot] + pv

    n_units = n_blk + 1
    score(*unit(0), 0, True)
    score(*unit(1), 1, True)

    def stages(t, buf):
        accumulate(*unit(t), buf)
        score(*unit(t + 2), buf, False)

    n_loop = (n_units - 2) // ATTN_UNROLL * ATTN_UNROLL

    def body(it, carry):
        for k in range(ATTN_UNROLL):
            stages(it * ATTN_UNROLL + k, k % 2)
        return carry

    lax.fori_loop(0, n_loop // ATTN_UNROLL, body, 0)
    for t in range(n_loop, n_units - 2):
        stages(t, t % 2)
    for t in range(n_units - 2, n_units):
        accumulate(*unit(t), t % 2)

    for slot in range(2):
        outs = []
        for hh in heads:
            acc = acc_refs[hh][slot]
            outs.append(acc[:HEAD_DIM] * (1.0 / acc[HEAD_DIM:HEAD_DIM + 1]))
        o_ref[1 - slot] = jnp.concatenate(outs, axis=0).T.astype(BF16)


def _moba(qt, k, vt, km, *, batch, seq, aw):
    n_blk = seq // MOBA_BLOCK
    n_h = ATTN_HEADS_PER_STEP
    width = n_h * HEAD_DIM
    kern = functools.partial(_moba_kernel, n_blk=n_blk)
    return pl.pallas_call(
        kern,
        grid=(batch, aw // width, n_blk // 2),
        in_specs=[pl.BlockSpec((None, None, width, MOBA_BLOCK), lambda b, hg, p: (b, p, hg, 0)),
                  pl.BlockSpec((None, None, width, MOBA_BLOCK), lambda b, hg, p: (b, n_blk - 1 - p, hg, 0)),
                  pl.BlockSpec((None, n_blk, width, MOBA_BLOCK), lambda b, hg, p: (b, 0, hg, 0)),
                  pl.BlockSpec((None, n_blk, width), lambda b, hg, p: (b, 0, hg)),
                  pl.BlockSpec((None, n_h, seq, LANES), lambda b, hg, p: (b, hg, 0, 0)),
                  pl.BlockSpec((None, n_blk, width, MOBA_BLOCK), lambda b, hg, p: (b, 0, hg, 0))],
        out_specs=pl.BlockSpec((None, 2, None, MOBA_BLOCK, width), lambda b, hg, p: (b, 0, p, 0, hg)),
        out_shape=jax.ShapeDtypeStruct((batch, 2, n_blk // 2, MOBA_BLOCK, aw), BF16),
        scratch_shapes=[pltpu.VMEM((n_blk, n_h, LANES - HEAD_DIM, MOBA_BLOCK), BF16),
                        pltpu.VMEM((2, n_h, LANES, MOBA_BLOCK), BF16),
                        pltpu.VMEM((2, n_h, MOBA_BLOCK, MOBA_BLOCK), F32),
                        pltpu.VMEM((2, n_h, 1, MOBA_BLOCK), F32)]
        + [pltpu.VMEM((2, 1, MOBA_BLOCK), F32)] * n_h
        + [pltpu.VMEM((2, HEAD_DIM + SUM_ROWS, MOBA_BLOCK), F32)] * n_h,
        compiler_params=_cparams(3),
        name="moba",
    )(qt, qt, qt, km, k, vt)


def _mixffn_kernel(h_ref, mod_ref, a_lo_ref, a_hi_ref, u_ref, vln_ref, ws_ref, bias_ref, ga_ref, gg_ref, wout_ref,
                   nw_ref, wgu_ref, wd_ref, nf_ref, o_ref, *, aw, gw, chunk, d_ff, tf, final_norm):
    tm = h_ref.shape[0]
    gt = mod_ref[0, 5:6, :]
    attn = jnp.concatenate([a_lo_ref[...], a_hi_ref[...]], axis=0).astype(F32)
    attn_n = _rmsnorm(attn, ga_ref[...]).astype(BF16)

    n_grp = gw // LANES
    row = lax.broadcasted_iota(jnp.int32, (HEADS_PER_SLAB * chunk, chunk), 0)
    col = lax.broadcasted_iota(jnp.int32, (HEADS_PER_SLAB * chunk, chunk), 1)
    keep = (row % chunk) >= col
    w_pairs = [jnp.where(keep, ws_ref[g], 0.0).astype(BF16) for g in range(n_grp)]
    lane = lax.broadcasted_iota(jnp.int32, (chunk, LANES), 1)
    first_head = lane < HEAD_DIM
    bias = bias_ref[...]

    rows = []
    for c in range(tm // chunk):
        vc = vln_ref[c * chunk:(c + 1) * chunk, :]
        mixed = []
        for g in range(n_grp):
            r = jnp.dot(w_pairs[g], vc[:, g * LANES:(g + 1) * LANES], preferred_element_type=F32)
            mixed.append(jnp.where(first_head, r[:chunk], r[chunk:]))
        mixed = jnp.concatenate(mixed, axis=1) + bias
        rows.append(u_ref[c * chunk:(c + 1) * chunk, :].astype(F32) * mixed)
    gm = jnp.concatenate(rows, axis=0)
    gmlp_n = _rmsnorm(gm, gg_ref[...]).astype(BF16)

    y = jnp.dot(attn_n, wout_ref[:aw, :], preferred_element_type=F32)
    y = y + jnp.dot(gmlp_n, wout_ref[aw:, :], preferred_element_type=F32)
    h2 = h_ref[...] + gt * y
    out = _swiglu_halfstep(h2, mod_ref, nw_ref, wgu_ref, wd_ref, mod_base=6, d_ff=d_ff, tf=tf)
    if final_norm:
        out = _rmsnorm(out, nf_ref[...])
    o_ref[...] = out


def _mixffn(h, mod, attn, u, vln, w_s, b_s, g_attn, g_gmlp, w_out, norm_w, w_gu, w_down, norm_final,
            *, seq, tm, final_norm):
    n_tok, D = h.shape
    aw, gw = g_attn.shape[0], g_gmlp.shape[0]
    n_gh, chunk = w_s.shape[0], w_s.shape[1]
    d_ff = w_down.shape[0]
    assert tm == 2 * MOBA_BLOCK
    st = seq // tm
    half = seq // MOBA_BLOCK // 2
    ws_pairs = w_s.reshape(n_gh // HEADS_PER_SLAB, HEADS_PER_SLAB * chunk, chunk)
    bias = jnp.repeat(b_s.T, gw // n_gh, axis=1)

    def attn_spec(which):
        def index(i):
            j = (i % st) * 2 + which
            return (i // st, j // half, jnp.where(j < half, j, 2 * half - 1 - j), 0, 0)
        return pl.BlockSpec((None, None, None, MOBA_BLOCK, aw), index)

    kern = functools.partial(_mixffn_kernel, aw=aw, gw=gw, chunk=chunk, d_ff=d_ff, tf=_ffn_tile(d_ff),
                             final_norm=final_norm)
    return pl.pallas_call(
        kern,
        grid=(n_tok // tm,),
        in_specs=[pl.BlockSpec((tm, D), lambda i: (i, 0)),
                  pl.BlockSpec((1, N_MOD, D), lambda i: (i // st, 0, 0)),
                  attn_spec(0),
                  attn_spec(1),
                  pl.BlockSpec((tm, gw), lambda i: (i, 0)),
                  pl.BlockSpec((tm, gw), lambda i: (i, 0)),
                  _resident(ws_pairs.shape),
                  _resident((chunk, gw)),
                  _resident((1, aw)),
                  _resident((1, gw)),
                  _resident((aw + gw, D)),
                  _resident((1, D)),
                  _resident((D, 2 * d_ff)),
                  _resident((d_ff, D)),
                  _resident((1, D))],
        out_specs=pl.BlockSpec((tm, D), lambda i: (i, 0)),
        out_shape=jax.ShapeDtypeStruct((n_tok, D), F32),
        compiler_params=_cparams(1),
        name="mixffn",
    )(h, mod, attn, attn, u, vln, ws_pairs, bias, g_attn.reshape(1, aw), g_gmlp.reshape(1, gw),
      w_out.astype(BF16), norm_w.reshape(1, D), w_gu.astype(BF16), w_down.astype(BF16), norm_final.reshape(1, D))


def kernel(x, c, w_ada, b_ada, norm_ffn1, w_ffn1_gu, w_ffn1_down, norm_mix, w_in, gmlp_ln_g, gmlp_ln_b,
           gmlp_w_s, gmlp_b_s, g_attn_out, g_gmlp_out, w_out, norm_ffn2, w_ffn2_gu, w_ffn2_down, norm_final):
    B, S, D = x.shape
    depth = w_ada.shape[0]
    aw, gw = g_attn_out.shape[1], g_gmlp_out.shape[1]
    n_blk = S // MOBA_BLOCK
    assert S % (2 * MOBA_BLOCK) == 0 and aw % (ATTN_HEADS_PER_STEP * HEAD_DIM) == 0 and gw % LANES == 0
    assert n_blk <= LANES - HEAD_DIM and n_blk % 16 == 0
    tm = 2 * MOBA_BLOCK
    tm_ffn = 1024 if S % 1024 == 0 else tm
    h = x.reshape(B * S, D)
    for l in range(depth):
        last = l == depth - 1
        mod = _ada(c, w_ada[l], b_ada[l])
        h, (w_in_l, w_out_l, w_gu2, w_down2) = _ffn(
            h, mod, norm_ffn1[l], w_ffn1_gu[l], w_ffn1_down[l],
            [w_in[l], w_out[l], w_ffn2_gu[l], w_ffn2_down[l]], seq=S, mod_base=0, tm=tm_ffn)
        qt, k, vt, km, u, vln = _inproj(h, mod, norm_mix[l], w_in_l, gmlp_ln_g[l], gmlp_ln_b[l],
                                        batch=B, seq=S, aw=aw, gw=gw, tm=tm)
        attn = _moba(qt, k, vt, km, batch=B, seq=S, aw=aw)
        h = _mixffn(h, mod, attn, u, vln, gmlp_w_s[l], gmlp_b_s[l], g_attn_out[l], g_gmlp_out[l], w_out_l,
                    norm_ffn2[l], w_gu2, w_down2, norm_final, seq=S, tm=tm, final_norm=last)
    return h.reshape(B, S, D)
```

```python
import functools

import jax
import jax.numpy as jnp
import numpy as np
from jax import lax
from jax.experimental import pallas as pl
from jax.experimental.pallas import tpu as pltpu

HEAD_DIM = 64
MOBA_BLOCK = 256
MOBA_TOPK = 3
ROPE_THETA = 10000.0
NORM_EPS = 1e-6
N_MOD = 9
NEG_INF = -1e30
LANES = 128
HEADS_PER_SLAB = LANES // HEAD_DIM
ATTN_HEADS_PER_STEP = 4
ATTN_UNROLL = 14
SUM_ROWS = 16
KM_ROWS = 8
VMEM_LIMIT_BYTES = 56 * 2**20

F32 = jnp.float32
BF16 = jnp.bfloat16


def _cparams(n_axes):
    return pltpu.CompilerParams(dimension_semantics=("arbitrary",) * n_axes,
                                vmem_limit_bytes=VMEM_LIMIT_BYTES)


def _resident(shape):
    nd = len(shape)
    return pl.BlockSpec(shape, lambda *_: (0,) * nd, pipeline_mode=pl.Buffered(1))


def _rmsnorm(x, g):
    return x * lax.rsqrt(jnp.mean(x * x, axis=-1, keepdims=True) + NORM_EPS) * g


def _ada_kernel(ct_ref, w_ref, b_ref, o_ref, *, batch):
    ca_t = jax.nn.silu(ct_ref[...])
    o_ref[...] = jnp.zeros(o_ref.shape, F32)
    for b in range(batch):
        col = jnp.broadcast_to(ca_t[:, b:b + 1], (ca_t.shape[0], LANES))
        for t in range(w_ref.shape[1] // LANES):
            lanes = slice(t * LANES, (t + 1) * LANES)
            o_ref[b:b + 1, lanes] = jnp.sum(col * w_ref[:, lanes], axis=0, keepdims=True) + b_ref[:, lanes]


def _ada(c, w_ada, b_ada):
    B, D = c.shape
    n_out = w_ada.shape[1]
    rows = -(-B // 8) * 8
    tn = n_out // 8 if (n_out // 8) % LANES == 0 else n_out
    mod = pl.pallas_call(
        functools.partial(_ada_kernel, batch=B),
        grid=(n_out // tn,),
        in_specs=[pl.BlockSpec((D, B), lambda j: (0, 0)),
                  pl.BlockSpec((D, tn), lambda j: (0, j)),
                  pl.BlockSpec((1, tn), lambda j: (0, j))],
        out_specs=pl.BlockSpec((rows, tn), lambda j: (0, j)),
        out_shape=jax.ShapeDtypeStruct((rows, n_out), F32),
        compiler_params=_cparams(1),
        name="ada",
    )(c.T, w_ada, b_ada.reshape(1, n_out))
    return mod[:B].reshape(B, N_MOD, D)


def _swiglu_halfstep(x, mod_ref, nw_ref, wgu_ref, wd_ref, *, mod_base, d_ff, tf):
    sh = mod_ref[0, mod_base:mod_base + 1, :]
    sc = mod_ref[0, mod_base + 1:mod_base + 2, :]
    gt = mod_ref[0, mod_base + 2:mod_base + 3, :]
    y = (_rmsnorm(x, nw_ref[...]) * (1.0 + sc) + sh).astype(BF16)
    acc = jnp.zeros(x.shape, F32)
    for c in range(d_ff // tf):
        g = jnp.dot(y, wgu_ref[:, c * tf:(c + 1) * tf], preferred_element_type=F32)
        u = jnp.dot(y, wgu_ref[:, d_ff + c * tf:d_ff + (c + 1) * tf], preferred_element_type=F32)
        mid = (jax.nn.silu(g) * u).astype(BF16)
        acc = acc + jnp.dot(mid, wd_ref[c * tf:(c + 1) * tf, :], preferred_element_type=F32)
    return x + (0.5 * gt) * acc


def _ffn_tile(d_ff):
    return 256 if d_ff % 256 == 0 else d_ff


def _ffn_kernel(h_ref, mod_ref, nw_ref, wgu_ref, wd_ref, *rest, mod_base, d_ff, tf, n_casts):
    cast_in, o_ref, cast_out = rest[:n_casts], rest[n_casts], rest[n_casts + 1:]
    o_ref[...] = _swiglu_halfstep(h_ref[...], mod_ref, nw_ref, wgu_ref, wd_ref, mod_base=mod_base, d_ff=d_ff, tf=tf)
    for src, dst in zip(cast_in, cast_out):
        dst[...] = src[...].astype(BF16)


def _row_chunks(rows, steps):
    n = max(d for d in range(1, steps + 1) if rows % (16 * d) == 0)
    return n, rows // n


def _ffn(h, mod, norm_w, w_gu, w_down, stacked_weights, layer, *, seq, mod_base, tm):
    n_tok, D = h.shape
    d_ff = w_down.shape[0]
    steps = n_tok // tm
    in_chunks, out_chunks = [], []
    for w in stacked_weights:
        n, rows = _row_chunks(w.shape[1], steps)
        in_chunks.append(pl.BlockSpec((None, rows, w.shape[2]), lambda i, n=n: (layer, jnp.minimum(i, n - 1), 0)))
        out_chunks.append(pl.BlockSpec((rows, w.shape[2]), lambda i, n=n: (jnp.minimum(i, n - 1), 0)))
    kern = functools.partial(_ffn_kernel, mod_base=mod_base, d_ff=d_ff, tf=_ffn_tile(d_ff),
                             n_casts=len(stacked_weights))
    out, *casts = pl.pallas_call(
        kern,
        grid=(steps,),
        in_specs=[pl.BlockSpec((tm, D), lambda i: (i, 0)),
                  pl.BlockSpec((1, N_MOD, D), lambda i: (i * tm // seq, 0, 0)),
                  _resident((1, D)),
                  _resident((D, 2 * d_ff)),
                  _resident((d_ff, D))] + in_chunks,
        out_specs=[pl.BlockSpec((tm, D), lambda i: (i, 0))] + out_chunks,
        out_shape=[jax.ShapeDtypeStruct((n_tok, D), F32)]
        + [jax.ShapeDtypeStruct(w.shape[1:], BF16) for w in stacked_weights],
        compiler_params=_cparams(1),
        name="ffn",
    )(h, mod, norm_w.reshape(1, D), w_gu.astype(BF16), w_down.astype(BF16), *stacked_weights)
    return out, casts


def _rope(x, cos, sin_signed, first_half):
    partner = jnp.where(first_half, pltpu.roll(x, LANES - HEAD_DIM // 2, 1), pltpu.roll(x, HEAD_DIM // 2, 1))
    return x * cos + partner * sin_signed


def _inproj_kernel(h_ref, mod_ref, nw_ref, win_ref, cos_ref, sin_ref, lng_ref, lnb_ref,
                   qt_ref, k_ref, vt_ref, km_ref, u_ref, vln_ref, *, aw, gw, st):
    x = h_ref[...]
    sh = mod_ref[0, 3:4, :]
    sc = mod_ref[0, 4:5, :]
    y = (_rmsnorm(x, nw_ref[...]) * (1.0 + sc) + sh).astype(BF16)

    proj = jnp.dot(y, win_ref[...], preferred_element_type=F32)
    cos = cos_ref[...]
    sin_signed = sin_ref[...]
    tm = x.shape[0]
    bpt = tm // MOBA_BLOCK
    lane = lax.broadcasted_iota(jnp.int32, cos.shape, 1)
    first_half = (lane % HEAD_DIM) < (HEAD_DIM // 2)
    is_key_lane = lane < HEAD_DIM
    row = lax.broadcasted_iota(jnp.int32, cos.shape, 0)
    blk_of_row = (pl.program_id(0) % st) * bpt + row // MOBA_BLOCK
    blk_onehot = jnp.where(lane - HEAD_DIM == blk_of_row, 1.0, 0.0)
    scale = HEAD_DIM ** -0.5 * np.log2(np.e)
    km_ref[...] = jnp.zeros(km_ref.shape, F32)
    for s in range(aw // LANES):
        qs = _rope(proj[:, s * LANES:(s + 1) * LANES], cos, sin_signed, first_half) * scale
        ks = _rope(proj[:, aw + s * LANES:aw + (s + 1) * LANES], cos, sin_signed, first_half)
        k_ref[2 * s] = jnp.where(is_key_lane, ks, blk_onehot).astype(BF16)
        k_ref[2 * s + 1] = jnp.where(is_key_lane, pltpu.roll(ks, HEAD_DIM, 1), blk_onehot).astype(BF16)
        vs = proj[:, 2 * aw + s * LANES:2 * aw + (s + 1) * LANES]
        for sb in range(bpt):
            rows = slice(sb * MOBA_BLOCK, (sb + 1) * MOBA_BLOCK)
            km_ref[sb:sb + 1, s * LANES:(s + 1) * LANES] = jnp.mean(ks[rows], axis=0, keepdims=True)
            qt_ref[sb, s * LANES:(s + 1) * LANES, :] = qs[rows].T.astype(BF16)
            vt_ref[sb, s * LANES:(s + 1) * LANES, :] = vs[rows].T.astype(BF16)
    u_ref[...] = jax.nn.gelu(proj[:, 3 * aw:3 * aw + gw]).astype(BF16)
    gv = jax.nn.gelu(proj[:, 3 * aw + gw:3 * aw + 2 * gw])
    mu = jnp.mean(gv, axis=-1, keepdims=True)
    var = jnp.mean(jnp.square(gv - mu), axis=-1, keepdims=True)
    vln = (gv - mu) * lax.rsqrt(var + NORM_EPS) * lng_ref[...] + lnb_ref[...]
    vln_ref[...] = vln.astype(BF16)


def _rope_tables(seq):
    half = HEAD_DIM // 2
    pos = jnp.arange(seq, dtype=F32)
    inv_freq = ROPE_THETA ** (-jnp.arange(half, dtype=F32) / half)
    ang = pos[:, None] * inv_freq[None, :]
    cos, sin = jnp.cos(ang), jnp.sin(ang)
    return (jnp.tile(jnp.concatenate([cos, cos], axis=-1), (1, HEADS_PER_SLAB)),
            jnp.tile(jnp.concatenate([-sin, sin], axis=-1), (1, HEADS_PER_SLAB)))


def _inproj(h, mod, norm_w, w_in, ln_g, ln_b, *, batch, seq, aw, gw, tm):
    n_tok, D = h.shape
    n_heads = aw // HEAD_DIM
    st = seq // tm
    bpt = tm // MOBA_BLOCK
    n_blk = seq // MOBA_BLOCK
    cos_t, sin_t = _rope_tables(seq)
    kern = functools.partial(_inproj_kernel, aw=aw, gw=gw, st=st)
    qt, k, vt, km, u, vln = pl.pallas_call(
        kern,
        grid=(n_tok // tm,),
        in_specs=[pl.BlockSpec((tm, D), lambda i: (i, 0)),
                  pl.BlockSpec((1, N_MOD, D), lambda i: (i // st, 0, 0)),
                  _resident((1, D)),
                  _resident((D, 3 * aw + 2 * gw)),
                  pl.BlockSpec((tm, LANES), lambda i: (i % st, 0)),
                  pl.BlockSpec((tm, LANES), lambda i: (i % st, 0)),
                  _resident((1, gw)),
                  _resident((1, gw))],
        out_specs=[pl.BlockSpec((None, bpt, aw, MOBA_BLOCK), lambda i: (i // st, i % st, 0, 0)),
                   pl.BlockSpec((None, n_heads, tm, LANES), lambda i: (i // st, 0, i % st, 0)),
                   pl.BlockSpec((None, bpt, aw, MOBA_BLOCK), lambda i: (i // st, i % st, 0, 0)),
                   pl.BlockSpec((None, None, KM_ROWS, aw), lambda i: (i // st, i % st, 0, 0)),
                   pl.BlockSpec((tm, gw), lambda i: (i, 0)),
                   pl.BlockSpec((tm, gw), lambda i: (i, 0))],
        out_shape=[jax.ShapeDtypeStruct((batch, n_blk, aw, MOBA_BLOCK), BF16),
                   jax.ShapeDtypeStruct((batch, n_heads, seq, LANES), BF16),
                   jax.ShapeDtypeStruct((batch, n_blk, aw, MOBA_BLOCK), BF16),
                   jax.ShapeDtypeStruct((batch, st, KM_ROWS, aw), F32),
                   jax.ShapeDtypeStruct((n_tok, gw), BF16),
                   jax.ShapeDtypeStruct((n_tok, gw), BF16)],
        compiler_params=_cparams(1),
        name="inproj",
    )(h, mod, norm_w.reshape(1, D), w_in.astype(BF16), cos_t, sin_t, ln_g.reshape(1, gw), ln_b.reshape(1, gw))
    km = km[:, :, :bpt, :].reshape(batch, n_blk, aw)
    return qt, k, vt, km, u, vln


def _moba_kernel(qa_ref, qb_ref, qt_all_ref, km_ref, k_ref, vt_ref, o_ref, qtail_ref, qaug_ref, s_ref, cmax_ref,
                 *state_refs, n_blk):
    p = pl.program_id(2)
    blk = MOBA_BLOCK
    n_h = ATTN_HEADS_PER_STEP
    heads = range(n_h)
    m_refs, acc_refs = state_refs[:n_h], state_refs[n_h:]
    zeros_half = jnp.zeros((HEAD_DIM, blk), BF16)

    @pl.when(p == 0)
    def _():
        blk_id = lax.broadcasted_iota(jnp.int32, (n_blk, blk), 0)
        bias_pad = jnp.zeros((LANES - HEAD_DIM - n_blk, blk), BF16)
        km_pieces = []
        for slab in range(n_h // HEADS_PER_SLAB):
            rest = km_ref[:, slab * LANES:(slab + 1) * LANES]
            pieces = []
            for _ in range(3):
                piece = rest.astype(BF16)
                rest = rest - piece.astype(F32)
                pieces.append(piece)
            km_pieces.append(pieces)

        def gate_body(ib, carry):
            for hh in heads:
                qt = qt_all_ref[ib, hh * HEAD_DIM:(hh + 1) * HEAD_DIM, :]
                halves = [zeros_half, qt] if hh % HEADS_PER_SLAB else [qt, zeros_half]
                q_slab = jnp.concatenate(halves, axis=0)
                gate = sum(jnp.dot(piece, q_slab, preferred_element_type=F32)
                           for piece in reversed(km_pieces[hh // HEADS_PER_SLAB]))
                gate = jnp.where(blk_id < ib, gate, NEG_INF)
                bias = jnp.where(blk_id == ib, 0.0, NEG_INF)
                for r in range(min(MOBA_TOPK, n_blk)):
                    top = jnp.max(gate, axis=0, keepdims=True)
                    idx = jnp.min(jnp.where(gate == top, blk_id, n_blk), axis=0, keepdims=True)
                    hit = blk_id == idx
                    rank_bias = jnp.where(r < ib, 0.0, NEG_INF)
                    bias = jnp.where(hit, jnp.maximum(bias, rank_bias), bias)
                    gate = jnp.where(hit, -jnp.inf, gate)
                qtail_ref[ib, hh] = jnp.concatenate([bias.astype(BF16), bias_pad], axis=0)
            return carry

        lax.fori_loop(0, n_blk, gate_body, 0, unroll=2)

    blk_a, blk_b = p, n_blk - 1 - p
    ones_rows = jnp.ones((SUM_ROWS, blk), BF16)
    key_pos = lax.broadcasted_iota(jnp.int32, (blk, blk), 0)
    q_pos = lax.broadcasted_iota(jnp.int32, (blk, blk), 1)
    causal = key_pos <= q_pos

    for slot, (q_ref, own) in enumerate(((qb_ref, blk_b), (qa_ref, blk_a))):
        for hh in heads:
            qaug_ref[slot, hh] = jnp.concatenate(
                [q_ref[hh * HEAD_DIM:(hh + 1) * HEAD_DIM, :], qtail_ref[own, hh]], axis=0)
            m_refs[hh][slot] = jnp.full((1, blk), NEG_INF, F32)
            acc_refs[hh][slot] = jnp.zeros((HEAD_DIM + SUM_ROWS, blk), F32)

    def unit(t):
        n = t - 2
        slot = jnp.where(t < 2, t, jnp.where(n >= blk_b, 1, 0))
        j = jnp.where(t == 0, blk_b, jnp.where(t == 1, blk_a, jnp.where(n >= blk_b, n - blk_b, n)))
        return slot, j

    def score(slot, j, buf, own):
        start = pl.multiple_of(j * blk, blk)
        for hh in heads:
            s = jnp.dot(k_ref[hh, pl.ds(start, blk), :], qaug_ref[slot, hh], preferred_element_type=F32)
            if own:
                s = jnp.where(causal, s, NEG_INF)
            s_ref[buf, hh] = s
            cmax_ref[buf, hh] = jnp.max(s, axis=0, keepdims=True)

    def accumulate(slot, j, buf):
        for hh in heads:
            m_old = m_refs[hh][slot]
            m_new = jnp.maximum(m_old, cmax_ref[buf, hh])
            e = jnp.exp2(s_ref[buf, hh] - m_new).astype(BF16)
            vt = jnp.concatenate([vt_ref[j, hh * HEAD_DIM:(hh + 1) * HEAD_DIM, :], ones_rows], axis=0)
            pv = jnp.dot(vt, e, preferred_element_type=F32)
            m_refs[hh][slot] = m_new
            acc_refs[hh][slot] = jnp.exp2(m_old - m_new) * acc_refs[hh][slot] + pv

    n_units = n_blk + 1
    score(*unit(0), 0, True)
    score(*unit(1), 1, True)

    def stages(t, buf):
        accumulate(*unit(t), buf)
        score(*unit(t + 2), buf, False)

    n_loop = (n_units - 2) // ATTN_UNROLL * ATTN_UNROLL

    def body(it, carry):
        for k in range(ATTN_UNROLL):
            stages(it * ATTN_UNROLL + k, k % 2)
        return carry

    lax.fori_loop(0, n_loop // ATTN_UNROLL, body, 0)
    for t in range(n_loop, n_units - 2):
        stages(t, t % 2)
    for t in range(n_units - 2, n_units):
        accumulate(*unit(t), t % 2)

    for slot in range(2):
        outs = []
        for hh in heads:
            acc = acc_refs[hh][slot]
            outs.append(acc[:HEAD_DIM] * (1.0 / acc[HEAD_DIM:HEAD_DIM + 1]))
        o_ref[1 - slot] = jnp.concatenate(outs, axis=0).T.astype(BF16)


def _moba(qt, k, vt, km, *, batch, seq, aw):
    n_blk = seq // MOBA_BLOCK
    n_h = ATTN_HEADS_PER_STEP
    width = n_h * HEAD_DIM
    kern = functools.partial(_moba_kernel, n_blk=n_blk)
    return pl.pallas_call(
        kern,
        grid=(batch, aw // width, n_blk // 2),
        in_specs=[pl.BlockSpec((None, None, width, MOBA_BLOCK), lambda b, hg, p: (b, p, hg, 0)),
                  pl.BlockSpec((None, None, width, MOBA_BLOCK), lambda b, hg, p: (b, n_blk - 1 - p, hg, 0)),
                  pl.BlockSpec((None, n_blk, width, MOBA_BLOCK), lambda b, hg, p: (b, 0, hg, 0)),
                  pl.BlockSpec((None, n_blk, width), lambda b, hg, p: (b, 0, hg)),
                  pl.BlockSpec((None, n_h, seq, LANES), lambda b, hg, p: (b, hg, 0, 0)),
                  pl.BlockSpec((None, n_blk, width, MOBA_BLOCK), lambda b, hg, p: (b, 0, hg, 0))],
        out_specs=pl.BlockSpec((None, 2, None, MOBA_BLOCK, width), lambda b, hg, p: (b, 0, p, 0, hg)),
        out_shape=jax.ShapeDtypeStruct((batch, 2, n_blk // 2, MOBA_BLOCK, aw), BF16),
        scratch_shapes=[pltpu.VMEM((n_blk, n_h, LANES - HEAD_DIM, MOBA_BLOCK), BF16),
                        pltpu.VMEM((2, n_h, LANES, MOBA_BLOCK), BF16),
                        pltpu.VMEM((2, n_h, MOBA_BLOCK, MOBA_BLOCK), F32),
                        pltpu.VMEM((2, n_h, 1, MOBA_BLOCK), F32)]
        + [pltpu.VMEM((2, 1, MOBA_BLOCK), F32)] * n_h
        + [pltpu.VMEM((2, HEAD_DIM + SUM_ROWS, MOBA_BLOCK), F32)] * n_h,
        compiler_params=_cparams(3),
        name="moba",
    )(qt, qt, qt, km, k, vt)


def _mixffn_kernel(h_ref, mod_ref, a_lo_ref, a_hi_ref, u_ref, vln_ref, ws_ref, bias_ref, ga_ref, gg_ref, wout_ref,
                   nw_ref, wgu_ref, wd_ref, nf_ref, o_ref, *, aw, gw, chunk, d_ff, tf, final_norm):
    tm = h_ref.shape[0]
    gt = mod_ref[0, 5:6, :]
    attn = jnp.concatenate([a_lo_ref[...], a_hi_ref[...]], axis=0).astype(F32)
    attn_n = _rmsnorm(attn, ga_ref[...]).astype(BF16)

    n_grp = gw // LANES
    row = lax.broadcasted_iota(jnp.int32, (HEADS_PER_SLAB * chunk, chunk), 0)
    col = lax.broadcasted_iota(jnp.int32, (HEADS_PER_SLAB * chunk, chunk), 1)
    keep = (row % chunk) >= col
    w_pairs = [jnp.where(keep, ws_ref[g], 0.0).astype(BF16) for g in range(n_grp)]
    lane = lax.broadcasted_iota(jnp.int32, (chunk, LANES), 1)
    first_head = lane < HEAD_DIM
    bias = bias_ref[...]

    rows = []
    for c in range(tm // chunk):
        vc = vln_ref[c * chunk:(c + 1) * chunk, :]
        mixed = []
        for g in range(n_grp):
            r = jnp.dot(w_pairs[g], vc[:, g * LANES:(g + 1) * LANES], preferred_element_type=F32)
            mixed.append(jnp.where(first_head, r[:chunk], r[chunk:]))
        mixed = jnp.concatenate(mixed, axis=1) + bias
        rows.append(u_ref[c * chunk:(c + 1) * chunk, :].astype(F32) * mixed)
    gm = jnp.concatenate(rows, axis=0)
    gmlp_n = _rmsnorm(gm, gg_ref[...]).astype(BF16)

    y = jnp.dot(attn_n, wout_ref[:aw, :], preferred_element_type=F32)
    y = y + jnp.dot(gmlp_n, wout_ref[aw:, :], preferred_element_type=F32)
    h2 = h_ref[...] + gt * y
    out = _swiglu_halfstep(h2, mod_ref, nw_ref, wgu_ref, wd_ref, mod_base=6, d_ff=d_ff, tf=tf)
    if final_norm:
        out = _rmsnorm(out, nf_ref[...])
    o_ref[...] = out


def _mixffn(h, mod, attn, u, vln, w_s, b_s, g_attn, g_gmlp, w_out, norm_w, w_gu, w_down, norm_final,
            *, seq, tm, final_norm):
    n_tok, D = h.shape
    aw, gw = g_attn.shape[0], g_gmlp.shape[0]
    n_gh, chunk = w_s.shape[0], w_s.shape[1]
    d_ff = w_down.shape[0]
    assert tm == 2 * MOBA_BLOCK
    st = seq // tm
    half = seq // MOBA_BLOCK // 2
    ws_pairs = w_s.reshape(n_gh // HEADS_PER_SLAB, HEADS_PER_SLAB * chunk, chunk)
    bias = jnp.repeat(b_s.T, gw // n_gh, axis=1)

    def attn_spec(which):
        def index(i):
            j = (i % st) * 2 + which
            return (i // st, j // half, jnp.where(j < half, j, 2 * half - 1 - j), 0, 0)
        return pl.BlockSpec((None, None, None, MOBA_BLOCK, aw), index)

    kern = functools.partial(_mixffn_kernel, aw=aw, gw=gw, chunk=chunk, d_ff=d_ff, tf=_ffn_tile(d_ff),
                             final_norm=final_norm)
    return pl.pallas_call(
        kern,
        grid=(n_tok // tm,),
        in_specs=[pl.BlockSpec((tm, D), lambda i: (i, 0)),
                  pl.BlockSpec((1, N_MOD, D), lambda i: (i // st, 0, 0)),
                  attn_spec(0),
                  attn_spec(1),
                  pl.BlockSpec((tm, gw), lambda i: (i, 0)),
                  pl.BlockSpec((tm, gw), lambda i: (i, 0)),
                  _resident(ws_pairs.shape),
                  _resident((chunk, gw)),
                  _resident((1, aw)),
                  _resident((1, gw)),
                  _resident((aw + gw, D)),
                  _resident((1, D)),
                  _resident((D, 2 * d_ff)),
                  _resident((d_ff, D)),
                  _resident((1, D))],
        out_specs=pl.BlockSpec((tm, D), lambda i: (i, 0)),
        out_shape=jax.ShapeDtypeStruct((n_tok, D), F32),
        compiler_params=_cparams(1),
        name="mixffn",
    )(h, mod, attn, attn, u, vln, ws_pairs, bias, g_attn.reshape(1, aw), g_gmlp.reshape(1, gw),
      w_out.astype(BF16), norm_w.reshape(1, D), w_gu.astype(BF16), w_down.astype(BF16), norm_final.reshape(1, D))


def kernel(x, c, w_ada, b_ada, norm_ffn1, w_ffn1_gu, w_ffn1_down, norm_mix, w_in, gmlp_ln_g, gmlp_ln_b,
           gmlp_w_s, gmlp_b_s, g_attn_out, g_gmlp_out, w_out, norm_ffn2, w_ffn2_gu, w_ffn2_down, norm_final):
    B, S, D = x.shape
    depth = w_ada.shape[0]
    aw, gw = g_attn_out.shape[1], g_gmlp_out.shape[1]
    n_blk = S // MOBA_BLOCK
    assert S % (2 * MOBA_BLOCK) == 0 and aw % (ATTN_HEADS_PER_STEP * HEAD_DIM) == 0 and gw % LANES == 0
    assert n_blk <= LANES - HEAD_DIM and n_blk % 16 == 0
    tm = 2 * MOBA_BLOCK
    tm_ffn = 1024 if S % 1024 == 0 else tm
    h = x.reshape(B * S, D)
    for l in range(depth):
        last = l == depth - 1
        mod = _ada(c, w_ada[l], b_ada[l])
        h, (w_in_l, w_out_l, w_gu2, w_down2) = _ffn(
            h, mod, norm_ffn1[l], w_ffn1_gu[l], w_ffn1_down[l],
            [w_in, w_out, w_ffn2_gu, w_ffn2_down], l, seq=S, mod_base=0, tm=tm_ffn)
        qt, k, vt, km, u, vln = _inproj(h, mod, norm_mix[l], w_in_l, gmlp_ln_g[l], gmlp_ln_b[l],
                                        batch=B, seq=S, aw=aw, gw=gw, tm=tm)
        attn = _moba(qt, k, vt, km, batch=B, seq=S, aw=aw)
        h = _mixffn(h, mod, attn, u, vln, gmlp_w_s[l], gmlp_b_s[l], g_attn_out[l], g_gmlp_out[l], w_out_l,
                    norm_ffn2[l], w_gu2, w_down2, norm_final, seq=S, tm=tm, final_norm=last)
    return h.reshape(B, S, D)
```

```python
import functools

import jax
import jax.numpy as jnp
import numpy as np
from jax import lax
from jax.experimental import pallas as pl
from jax.experimental.pallas import tpu as pltpu

HEAD_DIM = 64
MOBA_BLOCK = 256
MOBA_TOPK = 3
ROPE_THETA = 10000.0
NORM_EPS = 1e-6
N_MOD = 9
NEG_INF = -1e30
LANES = 128
HEADS_PER_SLAB = LANES // HEAD_DIM
ATTN_HEADS_PER_STEP = 4
ATTN_UNROLL = 14
SUM_ROWS = 16
KM_ROWS = 8
VMEM_LIMIT_BYTES = 56 * 2**20

F32 = jnp.float32
BF16 = jnp.bfloat16


def _cparams(n_axes):
    return pltpu.CompilerParams(dimension_semantics=("arbitrary",) * n_axes,
                                vmem_limit_bytes=VMEM_LIMIT_BYTES)


def _resident(shape):
    nd = len(shape)
    return pl.BlockSpec(shape, lambda *_: (0,) * nd, pipeline_mode=pl.Buffered(1))


def _rmsnorm(x, g):
    return x * lax.rsqrt(jnp.mean(x * x, axis=-1, keepdims=True) + NORM_EPS) * g


def _cast_chunk_specs(stacked_weights, layer, steps):
    in_specs, out_specs = [], []
    for w in stacked_weights:
        n = max(d for d in range(1, steps + 1) if w.shape[1] % (16 * d) == 0)
        rows = w.shape[1] // n
        in_specs.append(pl.BlockSpec((None, rows, w.shape[2]), lambda i, n=n: (layer, jnp.minimum(i, n - 1), 0)))
        out_specs.append(pl.BlockSpec((rows, w.shape[2]), lambda i, n=n: (jnp.minimum(i, n - 1), 0)))
    out_shapes = [jax.ShapeDtypeStruct(w.shape[1:], BF16) for w in stacked_weights]
    return in_specs, out_specs, out_shapes


def _cast_chunks(cast_in, cast_out):
    for src, dst in zip(cast_in, cast_out):
        dst[...] = src[...].astype(BF16)


def _ada_kernel(ct_ref, w_ref, b_ref, *rest, batch, n_casts):
    cast_in, o_ref, cast_out = rest[:n_casts], rest[n_casts], rest[n_casts + 1:]
    ca_t = jax.nn.silu(ct_ref[...])
    o_ref[...] = jnp.zeros(o_ref.shape, F32)
    for b in range(batch):
        col = jnp.broadcast_to(ca_t[:, b:b + 1], (ca_t.shape[0], LANES))
        for t in range(w_ref.shape[1] // LANES):
            lanes = slice(t * LANES, (t + 1) * LANES)
            o_ref[b:b + 1, lanes] = jnp.sum(col * w_ref[:, lanes], axis=0, keepdims=True) + b_ref[:, lanes]
    _cast_chunks(cast_in, cast_out)


def _ada(c, w_ada, b_ada, stacked_weights, layer):
    B, D = c.shape
    n_out = w_ada.shape[1]
    rows = -(-B // 8) * 8
    tn = n_out // 8 if (n_out // 8) % LANES == 0 else n_out
    steps = n_out // tn
    cast_in, cast_out, cast_shapes = _cast_chunk_specs(stacked_weights, layer, steps)
    mod, *casts = pl.pallas_call(
        functools.partial(_ada_kernel, batch=B, n_casts=len(stacked_weights)),
        grid=(steps,),
        in_specs=[pl.BlockSpec((D, B), lambda j: (0, 0)),
                  pl.BlockSpec((D, tn), lambda j: (0, j)),
                  pl.BlockSpec((1, tn), lambda j: (0, j))] + cast_in,
        out_specs=[pl.BlockSpec((rows, tn), lambda j: (0, j))] + cast_out,
        out_shape=[jax.ShapeDtypeStruct((rows, n_out), F32)] + cast_shapes,
        compiler_params=_cparams(1),
        name="ada",
    )(c.T, w_ada, b_ada.reshape(1, n_out), *stacked_weights)
    return mod[:B].reshape(B, N_MOD, D), casts


def _swiglu_halfstep(x, mod_ref, nw_ref, wgu_ref, wd_ref, *, mod_base, d_ff, tf):
    sh = mod_ref[0, mod_base:mod_base + 1, :]
    sc = mod_ref[0, mod_base + 1:mod_base + 2, :]
    gt = mod_ref[0, mod_base + 2:mod_base + 3, :]
    y = (_rmsnorm(x, nw_ref[...]) * (1.0 + sc) + sh).astype(BF16)
    acc = jnp.zeros(x.shape, F32)
    for c in range(d_ff // tf):
        g = jnp.dot(y, wgu_ref[:, c * tf:(c + 1) * tf], preferred_element_type=F32)
        u = jnp.dot(y, wgu_ref[:, d_ff + c * tf:d_ff + (c + 1) * tf], preferred_element_type=F32)
        mid = (jax.nn.silu(g) * u).astype(BF16)
        acc = acc + jnp.dot(mid, wd_ref[c * tf:(c + 1) * tf, :], preferred_element_type=F32)
    return x + (0.5 * gt) * acc


def _ffn_tile(d_ff):
    return 256 if d_ff % 256 == 0 else d_ff


def _ffn_kernel(h_ref, mod_ref, nw_ref, wgu_ref, wd_ref, *rest, mod_base, d_ff, tf, n_casts):
    cast_in, o_ref, cast_out = rest[:n_casts], rest[n_casts], rest[n_casts + 1:]
    o_ref[...] = _swiglu_halfstep(h_ref[...], mod_ref, nw_ref, wgu_ref, wd_ref, mod_base=mod_base, d_ff=d_ff, tf=tf)
    _cast_chunks(cast_in, cast_out)


def _ffn(h, mod, norm_w, w_gu, w_down, stacked_weights, layer, *, seq, mod_base, tm):
    n_tok, D = h.shape
    d_ff = w_down.shape[0]
    steps = n_tok // tm
    cast_in, cast_out, cast_shapes = _cast_chunk_specs(stacked_weights, layer, steps)
    kern = functools.partial(_ffn_kernel, mod_base=mod_base, d_ff=d_ff, tf=_ffn_tile(d_ff),
                             n_casts=len(stacked_weights))
    out, *casts = pl.pallas_call(
        kern,
        grid=(steps,),
        in_specs=[pl.BlockSpec((tm, D), lambda i: (i, 0)),
                  pl.BlockSpec((1, N_MOD, D), lambda i: (i * tm // seq, 0, 0)),
                  _resident((1, D)),
                  _resident((D, 2 * d_ff)),
                  _resident((d_ff, D))] + cast_in,
        out_specs=[pl.BlockSpec((tm, D), lambda i: (i, 0))] + cast_out,
        out_shape=[jax.ShapeDtypeStruct((n_tok, D), F32)] + cast_shapes,
        compiler_params=_cparams(1),
        name="ffn",
    )(h, mod, norm_w.reshape(1, D), w_gu, w_down, *stacked_weights)
    return out, casts


def _rope(x, cos, sin_signed, first_half):
    partner = jnp.where(first_half, pltpu.roll(x, LANES - HEAD_DIM // 2, 1), pltpu.roll(x, HEAD_DIM // 2, 1))
    return x * cos + partner * sin_signed


def _inproj_kernel(h_ref, mod_ref, nw_ref, win_ref, cos_ref, sin_ref, lng_ref, lnb_ref,
                   qt_ref, k_ref, vt_ref, km_ref, u_ref, vln_ref, *, aw, gw, st):
    x = h_ref[...]
    sh = mod_ref[0, 3:4, :]
    sc = mod_ref[0, 4:5, :]
    y = (_rmsnorm(x, nw_ref[...]) * (1.0 + sc) + sh).astype(BF16)

    proj = jnp.dot(y, win_ref[...], preferred_element_type=F32)
    cos = cos_ref[...]
    sin_signed = sin_ref[...]
    tm = x.shape[0]
    bpt = tm // MOBA_BLOCK
    lane = lax.broadcasted_iota(jnp.int32, cos.shape, 1)
    first_half = (lane % HEAD_DIM) < (HEAD_DIM // 2)
    is_key_lane = lane < HEAD_DIM
    row = lax.broadcasted_iota(jnp.int32, cos.shape, 0)
    blk_of_row = (pl.program_id(0) % st) * bpt + row // MOBA_BLOCK
    blk_onehot = jnp.where(lane - HEAD_DIM == blk_of_row, 1.0, 0.0)
    scale = HEAD_DIM ** -0.5 * np.log2(np.e)
    km_ref[...] = jnp.zeros(km_ref.shape, F32)
    for s in range(aw // LANES):
        qs = _rope(proj[:, s * LANES:(s + 1) * LANES], cos, sin_signed, first_half) * scale
        ks = _rope(proj[:, aw + s * LANES:aw + (s + 1) * LANES], cos, sin_signed, first_half)
        k_ref[2 * s] = jnp.where(is_key_lane, ks, blk_onehot).astype(BF16)
        k_ref[2 * s + 1] = jnp.where(is_key_lane, pltpu.roll(ks, HEAD_DIM, 1), blk_onehot).astype(BF16)
        vs = proj[:, 2 * aw + s * LANES:2 * aw + (s + 1) * LANES]
        for sb in range(bpt):
            rows = slice(sb * MOBA_BLOCK, (sb + 1) * MOBA_BLOCK)
            km_ref[sb:sb + 1, s * LANES:(s + 1) * LANES] = jnp.mean(ks[rows], axis=0, keepdims=True)
            qt_ref[sb, s * LANES:(s + 1) * LANES, :] = qs[rows].T.astype(BF16)
            vt_ref[sb, s * LANES:(s + 1) * LANES, :] = vs[rows].T.astype(BF16)
    u_ref[...] = jax.nn.gelu(proj[:, 3 * aw:3 * aw + gw]).astype(BF16)
    gv = jax.nn.gelu(proj[:, 3 * aw + gw:3 * aw + 2 * gw])
    mu = jnp.mean(gv, axis=-1, keepdims=True)
    var = jnp.mean(jnp.square(gv - mu), axis=-1, keepdims=True)
    vln = (gv - mu) * lax.rsqrt(var + NORM_EPS) * lng_ref[...] + lnb_ref[...]
    vln_ref[...] = vln.astype(BF16)


def _rope_tables(seq):
    half = HEAD_DIM // 2
    pos = jnp.arange(seq, dtype=F32)
    inv_freq = ROPE_THETA ** (-jnp.arange(half, dtype=F32) / half)
    ang = pos[:, None] * inv_freq[None, :]
    cos, sin = jnp.cos(ang), jnp.sin(ang)
    return (jnp.tile(jnp.concatenate([cos, cos], axis=-1), (1, HEADS_PER_SLAB)),
            jnp.tile(jnp.concatenate([-sin, sin], axis=-1), (1, HEADS_PER_SLAB)))


def _inproj(h, mod, norm_w, w_in, ln_g, ln_b, *, batch, seq, aw, gw, tm):
    n_tok, D = h.shape
    n_heads = aw // HEAD_DIM
    st = seq // tm
    bpt = tm // MOBA_BLOCK
    n_blk = seq // MOBA_BLOCK
    cos_t, sin_t = _rope_tables(seq)
    kern = functools.partial(_inproj_kernel, aw=aw, gw=gw, st=st)
    qt, k, vt, km, u, vln = pl.pallas_call(
        kern,
        grid=(n_tok // tm,),
        in_specs=[pl.BlockSpec((tm, D), lambda i: (i, 0)),
                  pl.BlockSpec((1, N_MOD, D), lambda i: (i // st, 0, 0)),
                  _resident((1, D)),
                  _resident((D, 3 * aw + 2 * gw)),
                  pl.BlockSpec((tm, LANES), lambda i: (i % st, 0)),
                  pl.BlockSpec((tm, LANES), lambda i: (i % st, 0)),
                  _resident((1, gw)),
                  _resident((1, gw))],
        out_specs=[pl.BlockSpec((None, bpt, aw, MOBA_BLOCK), lambda i: (i // st, i % st, 0, 0)),
                   pl.BlockSpec((None, n_heads, tm, LANES), lambda i: (i // st, 0, i % st, 0)),
                   pl.BlockSpec((None, bpt, aw, MOBA_BLOCK), lambda i: (i // st, i % st, 0, 0)),
                   pl.BlockSpec((None, None, KM_ROWS, aw), lambda i: (i // st, i % st, 0, 0)),
                   pl.BlockSpec((tm, gw), lambda i: (i, 0)),
                   pl.BlockSpec((tm, gw), lambda i: (i, 0))],
        out_shape=[jax.ShapeDtypeStruct((batch, n_blk, aw, MOBA_BLOCK), BF16),
                   jax.ShapeDtypeStruct((batch, n_heads, seq, LANES), BF16),
                   jax.ShapeDtypeStruct((batch, n_blk, aw, MOBA_BLOCK), BF16),
                   jax.ShapeDtypeStruct((batch, st, KM_ROWS, aw), F32),
                   jax.ShapeDtypeStruct((n_tok, gw), BF16),
                   jax.ShapeDtypeStruct((n_tok, gw), BF16)],
        compiler_params=_cparams(1),
        name="inproj",
    )(h, mod, norm_w.reshape(1, D), w_in, cos_t, sin_t, ln_g.reshape(1, gw), ln_b.reshape(1, gw))
    km = km[:, :, :bpt, :].reshape(batch, n_blk, aw)
    return qt, k, vt, km, u, vln


def _moba_kernel(qa_ref, qb_ref, qt_all_ref, km_ref, k_ref, vt_ref, o_ref, qtail_ref, qaug_ref, s_ref, cmax_ref,
                 *state_refs, n_blk):
    p = pl.program_id(2)
    blk = MOBA_BLOCK
    n_h = ATTN_HEADS_PER_STEP
    heads = range(n_h)
    m_refs, acc_refs = state_refs[:n_h], state_refs[n_h:]
    zeros_half = jnp.zeros((HEAD_DIM, blk), BF16)

    @pl.when(p == 0)
    def _():
        blk_id = lax.broadcasted_iota(jnp.int32, (n_blk, blk), 0)
        bias_pad = jnp.zeros((LANES - HEAD_DIM - n_blk, blk), BF16)
        km_pieces = []
        for slab in range(n_h // HEADS_PER_SLAB):
            rest = km_ref[:, slab * LANES:(slab + 1) * LANES]
            pieces = []
            for _ in range(3):
                piece = rest.astype(BF16)
                rest = rest - piece.astype(F32)
                pieces.append(piece)
            km_pieces.append(pieces)

        def gate_body(ib, carry):
            for hh in heads:
                qt = qt_all_ref[ib, hh * HEAD_DIM:(hh + 1) * HEAD_DIM, :]
                halves = [zeros_half, qt] if hh % HEADS_PER_SLAB else [qt, zeros_half]
                q_slab = jnp.concatenate(halves, axis=0)
                gate = sum(jnp.dot(piece, q_slab, preferred_element_type=F32)
                           for piece in reversed(km_pieces[hh // HEADS_PER_SLAB]))
                gate = jnp.where(blk_id < ib, gate, NEG_INF)
                bias = jnp.where(blk_id == ib, 0.0, NEG_INF)
                for r in range(min(MOBA_TOPK, n_blk)):
                    top = jnp.max(gate, axis=0, keepdims=True)
                    idx = jnp.min(jnp.where(gate == top, blk_id, n_blk), axis=0, keepdims=True)
                    hit = blk_id == idx
                    rank_bias = jnp.where(r < ib, 0.0, NEG_INF)
                    bias = jnp.where(hit, jnp.maximum(bias, rank_bias), bias)
                    gate = jnp.where(hit, -jnp.inf, gate)
                qtail_ref[ib, hh] = jnp.concatenate([bias.astype(BF16), bias_pad], axis=0)
            return carry

        lax.fori_loop(0, n_blk, gate_body, 0, unroll=4)

    blk_a, blk_b = p, n_blk - 1 - p
    ones_rows = jnp.ones((SUM_ROWS, blk), BF16)
    key_pos = lax.broadcasted_iota(jnp.int32, (blk, blk), 0)
    q_pos = lax.broadcasted_iota(jnp.int32, (blk, blk), 1)
    causal = key_pos <= q_pos

    for slot, (q_ref, own) in enumerate(((qb_ref, blk_b), (qa_ref, blk_a))):
        for hh in heads:
            qaug_ref[slot, hh] = jnp.concatenate(
                [q_ref[hh * HEAD_DIM:(hh + 1) * HEAD_DIM, :], qtail_ref[own, hh]], axis=0)
            m_refs[hh][slot] = jnp.full((1, blk), NEG_INF, F32)
            acc_refs[hh][slot] = jnp.zeros((HEAD_DIM + SUM_ROWS, blk), F32)

    def unit(t):
        n = t - 2
        slot = jnp.where(t < 2, t, jnp.where(n >= blk_b, 1, 0))
        j = jnp.where(t == 0, blk_b, jnp.where(t == 1, blk_a, jnp.where(n >= blk_b, n - blk_b, n)))
        return slot, j

    def score(slot, j, buf, own):
        start = pl.multiple_of(j * blk, blk)
        for hh in heads:
            s = jnp.dot(k_ref[hh, pl.ds(start, blk), :], qaug_ref[slot, hh], preferred_element_type=F32)
            if own:
                s = jnp.where(causal, s, NEG_INF)
            s_ref[buf, hh] = s
            cmax_ref[buf, hh] = jnp.max(s, axis=0, keepdims=True)

    def accumulate(slot, j, buf):
        for hh in heads:
            m_old = m_refs[hh][slot]
            m_new = jnp.maximum(m_old, cmax_ref[buf, hh])
            e = jnp.exp2(s_ref[buf, hh] - m_new).astype(BF16)
            vt = jnp.concatenate([vt_ref[j, hh * HEAD_DIM:(hh + 1) * HEAD_DIM, :], ones_rows], axis=0)
            pv = jnp.dot(vt, e, preferred_element_type=F32)
            m_refs[hh][slot] = m_new
            acc_refs[hh][slot] = jnp.exp2(m_old - m_new) * acc_refs[hh][slot] + pv

    n_units = n_blk + 1
    score(*unit(0), 0, True)
    score(*unit(1), 1, True)

    def stages(t, buf):
        accumulate(*unit(t), buf)
        score(*unit(t + 2), buf, False)

    n_loop = (n_units - 2) // ATTN_UNROLL * ATTN_UNROLL

    def body(it, carry):
        for k in range(ATTN_UNROLL):
            stages(it * ATTN_UNROLL + k, k % 2)
        return carry

    lax.fori_loop(0, n_loop // ATTN_UNROLL, body, 0)
    for t in range(n_loop, n_units - 2):
        stages(t, t % 2)
    for t in range(n_units - 2, n_units):
        accumulate(*unit(t), t % 2)

    for slot in range(2):
        outs = []
        for hh in heads:
            acc = acc_refs[hh][slot]
            outs.append(acc[:HEAD_DIM] * (1.0 / acc[HEAD_DIM:HEAD_DIM + 1]))
        o_ref[1 - slot] = jnp.concatenate(outs, axis=0).T.astype(BF16)


def _moba(qt, k, vt, km, *, batch, seq, aw):
    n_blk = seq // MOBA_BLOCK
    n_h = ATTN_HEADS_PER_STEP
    width = n_h * HEAD_DIM
    kern = functools.partial(_moba_kernel, n_blk=n_blk)
    return pl.pallas_call(
        kern,
        grid=(batch, aw // width, n_blk // 2),
        in_specs=[pl.BlockSpec((None, None, width, MOBA_BLOCK), lambda b, hg, p: (b, p, hg, 0)),
                  pl.BlockSpec((None, None, width, MOBA_BLOCK), lambda b, hg, p: (b, n_blk - 1 - p, hg, 0)),
                  pl.BlockSpec((None, n_blk, width, MOBA_BLOCK), lambda b, hg, p: (b, 0, hg, 0)),
                  pl.BlockSpec((None, n_blk, width), lambda b, hg, p: (b, 0, hg)),
                  pl.BlockSpec((None, n_h, seq, LANES), lambda b, hg, p: (b, hg, 0, 0)),
                  pl.BlockSpec((None, n_blk, width, MOBA_BLOCK), lambda b, hg, p: (b, 0, hg, 0))],
        out_specs=pl.BlockSpec((None, 2, None, MOBA_BLOCK, width), lambda b, hg, p: (b, 0, p, 0, hg)),
        out_shape=jax.ShapeDtypeStruct((batch, 2, n_blk // 2, MOBA_BLOCK, aw), BF16),
        scratch_shapes=[pltpu.VMEM((n_blk, n_h, LANES - HEAD_DIM, MOBA_BLOCK), BF16),
                        pltpu.VMEM((2, n_h, LANES, MOBA_BLOCK), BF16),
                        pltpu.VMEM((2, n_h, MOBA_BLOCK, MOBA_BLOCK), F32),
                        pltpu.VMEM((2, n_h, 1, MOBA_BLOCK), F32)]
        + [pltpu.VMEM((2, 1, MOBA_BLOCK), F32)] * n_h
        + [pltpu.VMEM((2, HEAD_DIM + SUM_ROWS, MOBA_BLOCK), F32)] * n_h,
        compiler_params=_cparams(3),
        name="moba",
    )(qt, qt, qt, km, k, vt)


def _mixffn_kernel(h_ref, mod_ref, *refs, n_attn, aw, gw, chunk, d_ff, tf, final_norm):
    attn_refs = refs[:n_attn]
    u_ref, vln_ref, ws_ref, bias_ref, ga_ref, gg_ref, wout_ref, nw_ref, wgu_ref, wd_ref, nf_ref, o_ref = refs[n_attn:]
    tm = h_ref.shape[0]
    gt = mod_ref[0, 5:6, :]
    attn = jnp.concatenate([a[...] for a in attn_refs], axis=0).astype(F32)
    attn_n = _rmsnorm(attn, ga_ref[...]).astype(BF16)

    n_grp = gw // LANES
    row = lax.broadcasted_iota(jnp.int32, (HEADS_PER_SLAB * chunk, chunk), 0)
    col = lax.broadcasted_iota(jnp.int32, (HEADS_PER_SLAB * chunk, chunk), 1)
    keep = (row % chunk) >= col
    w_pairs = [jnp.where(keep, ws_ref[g], 0.0).astype(BF16) for g in range(n_grp)]
    lane = lax.broadcasted_iota(jnp.int32, (chunk, LANES), 1)
    first_head = lane < HEAD_DIM
    bias = bias_ref[...]

    rows = []
    for c in range(tm // chunk):
        vc = vln_ref[c * chunk:(c + 1) * chunk, :]
        mixed = []
        for g in range(n_grp):
            r = jnp.dot(w_pairs[g], vc[:, g * LANES:(g + 1) * LANES], preferred_element_type=F32)
            mixed.append(jnp.where(first_head, r[:chunk], r[chunk:]))
        mixed = jnp.concatenate(mixed, axis=1) + bias
        rows.append(u_ref[c * chunk:(c + 1) * chunk, :].astype(F32) * mixed)
    gm = jnp.concatenate(rows, axis=0)
    gmlp_n = _rmsnorm(gm, gg_ref[...]).astype(BF16)

    y = jnp.dot(attn_n, wout_ref[:aw, :], preferred_element_type=F32)
    y = y + jnp.dot(gmlp_n, wout_ref[aw:, :], preferred_element_type=F32)
    h2 = h_ref[...] + gt * y
    out = _swiglu_halfstep(h2, mod_ref, nw_ref, wgu_ref, wd_ref, mod_base=6, d_ff=d_ff, tf=tf)
    if final_norm:
        out = _rmsnorm(out, nf_ref[...])
    o_ref[...] = out


def _mixffn(h, mod, attn, u, vln, w_s, b_s, g_attn, g_gmlp, w_out, norm_w, w_gu, w_down, norm_final,
            *, seq, tm, final_norm):
    n_tok, D = h.shape
    aw, gw = g_attn.shape[0], g_gmlp.shape[0]
    n_gh, chunk = w_s.shape[0], w_s.shape[1]
    d_ff = w_down.shape[0]
    st = seq // tm
    n_attn = tm // MOBA_BLOCK
    half = seq // MOBA_BLOCK // 2
    ws_pairs = w_s.reshape(n_gh // HEADS_PER_SLAB, HEADS_PER_SLAB * chunk, chunk)
    bias = jnp.repeat(b_s.T, gw // n_gh, axis=1)

    def attn_spec(which):
        def index(i):
            j = (i % st) * n_attn + which
            return (i // st, j // half, jnp.where(j < half, j, 2 * half - 1 - j), 0, 0)
        return pl.BlockSpec((None, None, None, MOBA_BLOCK, aw), index)

    kern = functools.partial(_mixffn_kernel, n_attn=n_attn, aw=aw, gw=gw, chunk=chunk, d_ff=d_ff,
                             tf=_ffn_tile(d_ff), final_norm=final_norm)
    return pl.pallas_call(
        kern,
        grid=(n_tok // tm,),
        in_specs=[pl.BlockSpec((tm, D), lambda i: (i, 0)),
                  pl.BlockSpec((1, N_MOD, D), lambda i: (i // st, 0, 0))]
        + [attn_spec(w) for w in range(n_attn)]
        + [pl.BlockSpec((tm, gw), lambda i: (i, 0)),
                  pl.BlockSpec((tm, gw), lambda i: (i, 0)),
                  _resident(ws_pairs.shape),
                  _resident((chunk, gw)),
                  _resident((1, aw)),
                  _resident((1, gw)),
                  _resident((aw + gw, D)),
                  _resident((1, D)),
                  _resident((D, 2 * d_ff)),
                  _resident((d_ff, D)),
                  _resident((1, D))],
        out_specs=pl.BlockSpec((tm, D), lambda i: (i, 0)),
        out_shape=jax.ShapeDtypeStruct((n_tok, D), F32),
        compiler_params=_cparams(1),
        name="mixffn",
    )(h, mod, *[attn] * n_attn, u, vln, ws_pairs, bias, g_attn.reshape(1, aw), g_gmlp.reshape(1, gw),
      w_out, norm_w.reshape(1, D), w_gu, w_down, norm_final.reshape(1, D))


def kernel(x, c, w_ada, b_ada, norm_ffn1, w_ffn1_gu, w_ffn1_down, norm_mix, w_in, gmlp_ln_g, gmlp_ln_b,
           gmlp_w_s, gmlp_b_s, g_attn_out, g_gmlp_out, w_out, norm_ffn2, w_ffn2_gu, w_ffn2_down, norm_final):
    B, S, D = x.shape
    depth = w_ada.shape[0]
    aw, gw = g_attn_out.shape[1], g_gmlp_out.shape[1]
    n_blk = S // MOBA_BLOCK
    assert S % (2 * MOBA_BLOCK) == 0 and aw % (ATTN_HEADS_PER_STEP * HEAD_DIM) == 0 and gw % LANES == 0
    assert n_blk <= LANES - HEAD_DIM and n_blk % 16 == 0
    tm = 2 * MOBA_BLOCK
    tm_ffn = 1024 if S % 1024 == 0 else tm
    h = x.reshape(B * S, D)
    for l in range(depth):
        last = l == depth - 1
        mod, (w_gu1, w_down1) = _ada(c, w_ada[l], b_ada[l], [w_ffn1_gu, w_ffn1_down], l)
        h, (w_in_l, w_out_l, w_gu2, w_down2) = _ffn(
            h, mod, norm_ffn1[l], w_gu1, w_down1,
            [w_in, w_out, w_ffn2_gu, w_ffn2_down], l, seq=S, mod_base=0, tm=tm_ffn)
        qt, k, vt, km, u, vln = _inproj(h, mod, norm_mix[l], w_in_l, gmlp_ln_g[l], gmlp_ln_b[l],
                                        batch=B, seq=S, aw=aw, gw=gw, tm=tm)
        attn = _moba(qt, k, vt, km, batch=B, seq=S, aw=aw)
        h = _mixffn(h, mod, attn, u, vln, gmlp_w_s[l], gmlp_b_s[l], g_attn_out[l], g_gmlp_out[l], w_out_l,
                    norm_ffn2[l], w_gu2, w_down2, norm_final, seq=S, tm=tm, final_norm=last)
    return h.reshape(B, S, D)
```

```python
import functools

import jax
import jax.numpy as jnp
import numpy as np
from jax import lax
from jax.experimental import pallas as pl
from jax.experimental.pallas import tpu as pltpu

HEAD_DIM = 64
MOBA_BLOCK = 256
MOBA_TOPK = 3
ROPE_THETA = 10000.0
NORM_EPS = 1e-6
N_MOD = 9
NEG_INF = -1e30
LANES = 128
HEADS_PER_SLAB = LANES // HEAD_DIM
ATTN_HEADS_PER_STEP = 4
ATTN_UNROLL = 14
SUM_ROWS = 16
KM_ROWS = 8
VMEM_LIMIT_BYTES = 56 * 2**20

F32 = jnp.float32
BF16 = jnp.bfloat16


def _cparams(n_axes):
    return pltpu.CompilerParams(dimension_semantics=("arbitrary",) * n_axes,
                                vmem_limit_bytes=VMEM_LIMIT_BYTES)


def _resident(shape):
    nd = len(shape)
    return pl.BlockSpec(shape, lambda *_: (0,) * nd, pipeline_mode=pl.Buffered(1))


def _rmsnorm(x, g):
    return x * lax.rsqrt(jnp.mean(x * x, axis=-1, keepdims=True) + NORM_EPS) * g


def _cast_chunk_specs(stacked_weights, layer, steps):
    in_specs, out_specs = [], []
    for w in stacked_weights:
        n = max(d for d in range(1, steps + 1) if w.shape[1] % (16 * d) == 0)
        rows = w.shape[1] // n
        in_specs.append(pl.BlockSpec((None, rows, w.shape[2]), lambda i, n=n: (layer, jnp.minimum(i, n - 1), 0)))
        out_specs.append(pl.BlockSpec((rows, w.shape[2]), lambda i, n=n: (jnp.minimum(i, n - 1), 0)))
    out_shapes = [jax.ShapeDtypeStruct(w.shape[1:], BF16) for w in stacked_weights]
    return in_specs, out_specs, out_shapes


def _cast_chunks(cast_in, cast_out):
    for src, dst in zip(cast_in, cast_out):
        dst[...] = src[...].astype(BF16)


def _ada_kernel(ct_ref, w_ref, b_ref, *rest, batch, n_casts):
    cast_in, o_ref, cast_out = rest[:n_casts], rest[n_casts], rest[n_casts + 1:]
    ca_t = jax.nn.silu(ct_ref[...])
    o_ref[...] = jnp.zeros(o_ref.shape, F32)
    for b in range(batch):
        col = jnp.broadcast_to(ca_t[:, b:b + 1], (ca_t.shape[0], LANES))
        for t in range(w_ref.shape[1] // LANES):
            lanes = slice(t * LANES, (t + 1) * LANES)
            o_ref[b:b + 1, lanes] = jnp.sum(col * w_ref[:, lanes], axis=0, keepdims=True) + b_ref[:, lanes]
    _cast_chunks(cast_in, cast_out)


def _ada(c, w_ada, b_ada, stacked_weights, layer):
    B, D = c.shape
    n_out = w_ada.shape[1]
    rows = -(-B // 8) * 8
    tn = n_out // 8 if (n_out // 8) % LANES == 0 else n_out
    steps = n_out // tn
    cast_in, cast_out, cast_shapes = _cast_chunk_specs(stacked_weights, layer, steps)
    mod, *casts = pl.pallas_call(
        functools.partial(_ada_kernel, batch=B, n_casts=len(stacked_weights)),
        grid=(steps,),
        in_specs=[pl.BlockSpec((D, B), lambda j: (0, 0)),
                  pl.BlockSpec((D, tn), lambda j: (0, j)),
                  pl.BlockSpec((1, tn), lambda j: (0, j))] + cast_in,
        out_specs=[pl.BlockSpec((rows, tn), lambda j: (0, j))] + cast_out,
        out_shape=[jax.ShapeDtypeStruct((rows, n_out), F32)] + cast_shapes,
        compiler_params=_cparams(1),
        name="ada",
    )(c.T, w_ada, b_ada.reshape(1, n_out), *stacked_weights)
    return mod[:B].reshape(B, N_MOD, D), casts


def _swiglu_halfstep(x, mod_ref, nw_ref, wgu_ref, wd_ref, *, mod_base, d_ff, tf):
    sh = mod_ref[0, mod_base:mod_base + 1, :]
    sc = mod_ref[0, mod_base + 1:mod_base + 2, :]
    gt = mod_ref[0, mod_base + 2:mod_base + 3, :]
    y = (_rmsnorm(x, nw_ref[...]) * (1.0 + sc) + sh).astype(BF16)
    acc = jnp.zeros(x.shape, F32)
    for c in range(d_ff // tf):
        g = jnp.dot(y, wgu_ref[:, c * tf:(c + 1) * tf], preferred_element_type=F32)
        u = jnp.dot(y, wgu_ref[:, d_ff + c * tf:d_ff + (c + 1) * tf], preferred_element_type=F32)
        mid = (jax.nn.silu(g) * u).astype(BF16)
        acc = acc + jnp.dot(mid, wd_ref[c * tf:(c + 1) * tf, :], preferred_element_type=F32)
    return x + (0.5 * gt) * acc


def _ffn_tile(d_ff):
    return 256 if d_ff % 256 == 0 else d_ff


def _ffn_kernel(h_ref, mod_ref, nw_ref, wgu_ref, wd_ref, *rest, mod_base, d_ff, tf, n_casts):
    cast_in, o_ref, cast_out = rest[:n_casts], rest[n_casts], rest[n_casts + 1:]
    o_ref[...] = _swiglu_halfstep(h_ref[...], mod_ref, nw_ref, wgu_ref, wd_ref, mod_base=mod_base, d_ff=d_ff, tf=tf)
    _cast_chunks(cast_in, cast_out)


def _ffn(h, mod, norm_w, w_gu, w_down, stacked_weights, layer, *, seq, mod_base, tm):
    n_tok, D = h.shape
    d_ff = w_down.shape[0]
    steps = n_tok // tm
    cast_in, cast_out, cast_shapes = _cast_chunk_specs(stacked_weights, layer, steps)
    kern = functools.partial(_ffn_kernel, mod_base=mod_base, d_ff=d_ff, tf=_ffn_tile(d_ff),
                             n_casts=len(stacked_weights))
    out, *casts = pl.pallas_call(
        kern,
        grid=(steps,),
        in_specs=[pl.BlockSpec((tm, D), lambda i: (i, 0)),
                  pl.BlockSpec((1, N_MOD, D), lambda i: (i * tm // seq, 0, 0)),
                  _resident((1, D)),
                  _resident((D, 2 * d_ff)),
                  _resident((d_ff, D))] + cast_in,
        out_specs=[pl.BlockSpec((tm, D), lambda i: (i, 0))] + cast_out,
        out_shape=[jax.ShapeDtypeStruct((n_tok, D), F32)] + cast_shapes,
        compiler_params=_cparams(1),
        name="ffn",
    )(h, mod, norm_w.reshape(1, D), w_gu, w_down, *stacked_weights)
    return out, casts


def _rope(x, cos, sin_signed, first_half):
    partner = jnp.where(first_half, pltpu.roll(x, LANES - HEAD_DIM // 2, 1), pltpu.roll(x, HEAD_DIM // 2, 1))
    return x * cos + partner * sin_signed


def _inproj_kernel(h_ref, mod_ref, nw_ref, win_ref, cos_ref, sin_ref, lng_ref, lnb_ref,
                   qt_ref, k_ref, vt_ref, km_ref, u_ref, vln_ref, *, aw, gw, st):
    x = h_ref[...]
    sh = mod_ref[0, 3:4, :]
    sc = mod_ref[0, 4:5, :]
    y = (_rmsnorm(x, nw_ref[...]) * (1.0 + sc) + sh).astype(BF16)

    proj = jnp.dot(y, win_ref[...], preferred_element_type=F32)
    cos = cos_ref[...]
    sin_signed = sin_ref[...]
    tm = x.shape[0]
    bpt = tm // MOBA_BLOCK
    lane = lax.broadcasted_iota(jnp.int32, cos.shape, 1)
    first_half = (lane % HEAD_DIM) < (HEAD_DIM // 2)
    is_key_lane = lane < HEAD_DIM
    row = lax.broadcasted_iota(jnp.int32, cos.shape, 0)
    blk_of_row = (pl.program_id(0) % st) * bpt + row // MOBA_BLOCK
    blk_onehot = jnp.where(lane - HEAD_DIM == blk_of_row, 1.0, 0.0)
    scale = HEAD_DIM ** -0.5 * np.log2(np.e)
    km_ref[...] = jnp.zeros(km_ref.shape, F32)
    for s in range(aw // LANES):
        qs = _rope(proj[:, s * LANES:(s + 1) * LANES], cos, sin_signed, first_half) * scale
        ks = _rope(proj[:, aw + s * LANES:aw + (s + 1) * LANES], cos, sin_signed, first_half)
        k_ref[2 * s] = jnp.where(is_key_lane, ks, blk_onehot).astype(BF16)
        k_ref[2 * s + 1] = jnp.where(is_key_lane, pltpu.roll(ks, HEAD_DIM, 1), blk_onehot).astype(BF16)
        vs = proj[:, 2 * aw + s * LANES:2 * aw + (s + 1) * LANES]
        for sb in range(bpt):
            rows = slice(sb * MOBA_BLOCK, (sb + 1) * MOBA_BLOCK)
            km_ref[sb:sb + 1, s * LANES:(s + 1) * LANES] = jnp.mean(ks[rows], axis=0, keepdims=True)
            qt_ref[sb, s * LANES:(s + 1) * LANES, :] = qs[rows].T.astype(BF16)
            vt_ref[sb, s * LANES:(s + 1) * LANES, :] = vs[rows].T.astype(BF16)
    u_ref[...] = jax.nn.gelu(proj[:, 3 * aw:3 * aw + gw]).astype(BF16)
    gv = jax.nn.gelu(proj[:, 3 * aw + gw:3 * aw + 2 * gw])
    mu = jnp.mean(gv, axis=-1, keepdims=True)
    var = jnp.mean(jnp.square(gv - mu), axis=-1, keepdims=True)
    vln = (gv - mu) * lax.rsqrt(var + NORM_EPS) * lng_ref[...] + lnb_ref[...]
    vln_ref[...] = vln.astype(BF16)


def _rope_tables(seq):
    half = HEAD_DIM // 2
    pos = jnp.arange(seq, dtype=F32)
    inv_freq = ROPE_THETA ** (-jnp.arange(half, dtype=F32) / half)
    ang = pos[:, None] * inv_freq[None, :]
    cos, sin = jnp.cos(ang), jnp.sin(ang)
    return (jnp.tile(jnp.concatenate([cos, cos], axis=-1), (1, HEADS_PER_SLAB)),
            jnp.tile(jnp.concatenate([-sin, sin], axis=-1), (1, HEADS_PER_SLAB)))


def _inproj(h, mod, norm_w, w_in, ln_g, ln_b, *, batch, seq, aw, gw, tm):
    n_tok, D = h.shape
    n_heads = aw // HEAD_DIM
    st = seq // tm
    bpt = tm // MOBA_BLOCK
    n_blk = seq // MOBA_BLOCK
    cos_t, sin_t = _rope_tables(seq)
    kern = functools.partial(_inproj_kernel, aw=aw, gw=gw, st=st)
    qt, k, vt, km, u, vln = pl.pallas_call(
        kern,
        grid=(n_tok // tm,),
        in_specs=[pl.BlockSpec((tm, D), lambda i: (i, 0)),
                  pl.BlockSpec((1, N_MOD, D), lambda i: (i // st, 0, 0)),
                  _resident((1, D)),
                  _resident((D, 3 * aw + 2 * gw)),
                  pl.BlockSpec((tm, LANES), lambda i: (i % st, 0)),
                  pl.BlockSpec((tm, LANES), lambda i: (i % st, 0)),
                  _resident((1, gw)),
                  _resident((1, gw))],
        out_specs=[pl.BlockSpec((None, bpt, aw, MOBA_BLOCK), lambda i: (i // st, i % st, 0, 0)),
                   pl.BlockSpec((None, n_heads, tm, LANES), lambda i: (i // st, 0, i % st, 0)),
                   pl.BlockSpec((None, bpt, aw, MOBA_BLOCK), lambda i: (i // st, i % st, 0, 0)),
                   pl.BlockSpec((None, None, KM_ROWS, aw), lambda i: (i // st, i % st, 0, 0)),
                   pl.BlockSpec((tm, gw), lambda i: (i, 0)),
                   pl.BlockSpec((tm, gw), lambda i: (i, 0))],
        out_shape=[jax.ShapeDtypeStruct((batch, n_blk, aw, MOBA_BLOCK), BF16),
                   jax.ShapeDtypeStruct((batch, n_heads, seq, LANES), BF16),
                   jax.ShapeDtypeStruct((batch, n_blk, aw, MOBA_BLOCK), BF16),
                   jax.ShapeDtypeStruct((batch, st, KM_ROWS, aw), F32),
                   jax.ShapeDtypeStruct((n_tok, gw), BF16),
                   jax.ShapeDtypeStruct((n_tok, gw), BF16)],
        compiler_params=_cparams(1),
        name="inproj",
    )(h, mod, norm_w.reshape(1, D), w_in, cos_t, sin_t, ln_g.reshape(1, gw), ln_b.reshape(1, gw))
    km = km[:, :, :bpt, :].reshape(batch, n_blk, aw)
    return qt, k, vt, km, u, vln


def _moba_kernel(qa_ref, qb_ref, qt_all_ref, km_ref, k_ref, vt_ref, o_ref, qtail_ref, qaug_ref, s_ref, cmax_ref,
                 *state_refs, n_blk):
    p = pl.program_id(2)
    blk = MOBA_BLOCK
    n_h = ATTN_HEADS_PER_STEP
    heads = range(n_h)
    m_refs, acc_refs = state_refs[:n_h], state_refs[n_h:]
    zeros_half = jnp.zeros((HEAD_DIM, blk), BF16)

    @pl.when(p == 0)
    def _():
        blk_id = lax.broadcasted_iota(jnp.int32, (n_blk, blk), 0)
        bias_pad = jnp.zeros((LANES - HEAD_DIM - n_blk, blk), BF16)
        km_pieces = []
        for slab in range(n_h // HEADS_PER_SLAB):
            rest = km_ref[:, slab * LANES:(slab + 1) * LANES]
            pieces = []
            for _ in range(3):
                piece = rest.astype(BF16)
                rest = rest - piece.astype(F32)
                pieces.append(piece)
            km_pieces.append(pieces)

        def gate_body(ib, carry):
            for hh in heads:
                qt = qt_all_ref[ib, hh * HEAD_DIM:(hh + 1) * HEAD_DIM, :]
                halves = [zeros_half, qt] if hh % HEADS_PER_SLAB else [qt, zeros_half]
                q_slab = jnp.concatenate(halves, axis=0)
                gate = sum(jnp.dot(piece, q_slab, preferred_element_type=F32)
                           for piece in reversed(km_pieces[hh // HEADS_PER_SLAB]))
                gate = jnp.where(blk_id < ib, gate, NEG_INF)
                bias = jnp.where(blk_id == ib, 0.0, NEG_INF)
                for r in range(min(MOBA_TOPK, n_blk)):
                    top = jnp.max(gate, axis=0, keepdims=True)
                    idx = jnp.min(jnp.where(gate == top, blk_id, n_blk), axis=0, keepdims=True)
                    hit = blk_id == idx
                    rank_bias = jnp.where(r < ib, 0.0, NEG_INF)
                    bias = jnp.where(hit, jnp.maximum(bias, rank_bias), bias)
                    gate = jnp.where(hit, -jnp.inf, gate)
                qtail_ref[ib, hh] = jnp.concatenate([bias.astype(BF16), bias_pad], axis=0)
            return carry

        lax.fori_loop(0, n_blk, gate_body, 0, unroll=4)

    blk_a, blk_b = p, n_blk - 1 - p
    ones_rows = jnp.ones((SUM_ROWS, blk), BF16)
    key_pos = lax.broadcasted_iota(jnp.int32, (blk, blk), 0)
    q_pos = lax.broadcasted_iota(jnp.int32, (blk, blk), 1)
    causal = key_pos <= q_pos

    for slot, (q_ref, own) in enumerate(((qb_ref, blk_b), (qa_ref, blk_a))):
        for hh in heads:
            qaug_ref[slot, hh] = jnp.concatenate(
                [q_ref[hh * HEAD_DIM:(hh + 1) * HEAD_DIM, :], qtail_ref[own, hh]], axis=0)
            m_refs[hh][slot] = jnp.full((1, blk), NEG_INF, F32)
            acc_refs[hh][slot] = jnp.zeros((HEAD_DIM + SUM_ROWS, blk), F32)

    def unit(t):
        n = t - 2
        slot = jnp.where(t < 2, t, jnp.where(n >= blk_b, 1, 0))
        j = jnp.where(t == 0, blk_b, jnp.where(t == 1, blk_a, jnp.where(n >= blk_b, n - blk_b, n)))
        return slot, j

    def score(hh, slot, j, buf, own):
        start = pl.multiple_of(j * blk, blk)
        s = jnp.dot(k_ref[hh, pl.ds(start, blk), :], qaug_ref[slot, hh], preferred_element_type=F32)
        if own:
            s = jnp.where(causal, s, NEG_INF)
        s_ref[buf, hh] = s
        cmax_ref[buf, hh] = jnp.max(s, axis=0, keepdims=True)

    def accumulate(hh, slot, j, buf):
        m_old = m_refs[hh][slot]
        m_new = jnp.maximum(m_old, cmax_ref[buf, hh])
        e = jnp.exp2(s_ref[buf, hh] - m_new).astype(BF16)
        vt = jnp.concatenate([vt_ref[j, hh * HEAD_DIM:(hh + 1) * HEAD_DIM, :], ones_rows], axis=0)
        pv = jnp.dot(vt, e, preferred_element_type=F32)
        m_refs[hh][slot] = m_new
        acc_refs[hh][slot] = jnp.exp2(m_old - m_new) * acc_refs[hh][slot] + pv

    n_units = n_blk + 1
    for t in range(2):
        for hh in heads:
            score(hh, *unit(t), t, True)

    def stages(t, buf):
        now, ahead = unit(t), unit(t + 2)
        for hh in heads:
            accumulate(hh, *now, buf)
            score(hh, *ahead, buf, False)

    n_loop = (n_units - 2) // ATTN_UNROLL * ATTN_UNROLL

    def body(it, carry):
        for k in range(ATTN_UNROLL):
            stages(it * ATTN_UNROLL + k, k % 2)
        return carry

    lax.fori_loop(0, n_loop // ATTN_UNROLL, body, 0)
    for t in range(n_loop, n_units - 2):
        stages(t, t % 2)
    for t in range(n_units - 2, n_units):
        for hh in heads:
            accumulate(hh, *unit(t), t % 2)

    for slot in range(2):
        outs = []
        for hh in heads:
            acc = acc_refs[hh][slot]
            outs.append(acc[:HEAD_DIM] * (1.0 / acc[HEAD_DIM:HEAD_DIM + 1]))
        o_ref[1 - slot] = jnp.concatenate(outs, axis=0).T.astype(BF16)


def _moba(qt, k, vt, km, *, batch, seq, aw):
    n_blk = seq // MOBA_BLOCK
    n_h = ATTN_HEADS_PER_STEP
    width = n_h * HEAD_DIM
    kern = functools.partial(_moba_kernel, n_blk=n_blk)
    return pl.pallas_call(
        kern,
        grid=(batch, aw // width, n_blk // 2),
        in_specs=[pl.BlockSpec((None, None, width, MOBA_BLOCK), lambda b, hg, p: (b, p, hg, 0)),
                  pl.BlockSpec((None, None, width, MOBA_BLOCK), lambda b, hg, p: (b, n_blk - 1 - p, hg, 0)),
                  pl.BlockSpec((None, n_blk, width, MOBA_BLOCK), lambda b, hg, p: (b, 0, hg, 0)),
                  pl.BlockSpec((None, n_blk, width), lambda b, hg, p: (b, 0, hg)),
                  pl.BlockSpec((None, n_h, seq, LANES), lambda b, hg, p: (b, hg, 0, 0)),
                  pl.BlockSpec((None, n_blk, width, MOBA_BLOCK), lambda b, hg, p: (b, 0, hg, 0))],
        out_specs=pl.BlockSpec((None, 2, None, MOBA_BLOCK, width), lambda b, hg, p: (b, 0, p, 0, hg)),
        out_shape=jax.ShapeDtypeStruct((batch, 2, n_blk // 2, MOBA_BLOCK, aw), BF16),
        scratch_shapes=[pltpu.VMEM((n_blk, n_h, LANES - HEAD_DIM, MOBA_BLOCK), BF16),
                        pltpu.VMEM((2, n_h, LANES, MOBA_BLOCK), BF16),
                        pltpu.VMEM((2, n_h, MOBA_BLOCK, MOBA_BLOCK), F32),
                        pltpu.VMEM((2, n_h, 1, MOBA_BLOCK), F32)]
        + [pltpu.VMEM((2, 1, MOBA_BLOCK), F32)] * n_h
        + [pltpu.VMEM((2, HEAD_DIM + SUM_ROWS, MOBA_BLOCK), F32)] * n_h,
        compiler_params=_cparams(3),
        name="moba",
    )(qt, qt, qt, km, k, vt)


def _mixffn_kernel(h_ref, mod_ref, *refs, n_attn, aw, gw, chunk, d_ff, tf, final_norm):
    attn_refs = refs[:n_attn]
    u_ref, vln_ref, ws_ref, bias_ref, ga_ref, gg_ref, wout_ref, nw_ref, wgu_ref, wd_ref, nf_ref, o_ref = refs[n_attn:]
    tm = h_ref.shape[0]
    gt = mod_ref[0, 5:6, :]
    attn = jnp.concatenate([a[...] for a in attn_refs], axis=0).astype(F32)
    attn_n = _rmsnorm(attn, ga_ref[...]).astype(BF16)

    n_grp = gw // LANES
    row = lax.broadcasted_iota(jnp.int32, (HEADS_PER_SLAB * chunk, chunk), 0)
    col = lax.broadcasted_iota(jnp.int32, (HEADS_PER_SLAB * chunk, chunk), 1)
    keep = (row % chunk) >= col
    w_pairs = [jnp.where(keep, ws_ref[g], 0.0).astype(BF16) for g in range(n_grp)]
    lane = lax.broadcasted_iota(jnp.int32, (chunk, LANES), 1)
    first_head = lane < HEAD_DIM
    bias = bias_ref[...]

    rows = []
    for c in range(tm // chunk):
        vc = vln_ref[c * chunk:(c + 1) * chunk, :]
        mixed = []
        for g in range(n_grp):
            r = jnp.dot(w_pairs[g], vc[:, g * LANES:(g + 1) * LANES], preferred_element_type=F32)
            mixed.append(jnp.where(first_head, r[:chunk], r[chunk:]))
        mixed = jnp.concatenate(mixed, axis=1) + bias
        rows.append(u_ref[c * chunk:(c + 1) * chunk, :].astype(F32) * mixed)
    gm = jnp.concatenate(rows, axis=0)
    gmlp_n = _rmsnorm(gm, gg_ref[...]).astype(BF16)

    y = jnp.dot(attn_n, wout_ref[:aw, :], preferred_element_type=F32)
    y = y + jnp.dot(gmlp_n, wout_ref[aw:, :], preferred_element_type=F32)
    h2 = h_ref[...] + gt * y
    out = _swiglu_halfstep(h2, mod_ref, nw_ref, wgu_ref, wd_ref, mod_base=6, d_ff=d_ff, tf=tf)
    if final_norm:
        out = _rmsnorm(out, nf_ref[...])
    o_ref[...] = out


def _mixffn(h, mod, attn, u, vln, w_s, b_s, g_attn, g_gmlp, w_out, norm_w, w_gu, w_down, norm_final,
            *, seq, tm, final_norm):
    n_tok, D = h.shape
    aw, gw = g_attn.shape[0], g_gmlp.shape[0]
    n_gh, chunk = w_s.shape[0], w_s.shape[1]
    d_ff = w_down.shape[0]
    st = seq // tm
    n_attn = tm // MOBA_BLOCK
    half = seq // MOBA_BLOCK // 2
    ws_pairs = w_s.reshape(n_gh // HEADS_PER_SLAB, HEADS_PER_SLAB * chunk, chunk)
    bias = jnp.repeat(b_s.T, gw // n_gh, axis=1)

    def attn_spec(which):
        def index(i):
            j = (i % st) * n_attn + which
            return (i // st, j // half, jnp.where(j < half, j, 2 * half - 1 - j), 0, 0)
        return pl.BlockSpec((None, None, None, MOBA_BLOCK, aw), index)

    kern = functools.partial(_mixffn_kernel, n_attn=n_attn, aw=aw, gw=gw, chunk=chunk, d_ff=d_ff,
                             tf=_ffn_tile(d_ff), final_norm=final_norm)
    return pl.pallas_call(
        kern,
        grid=(n_tok // tm,),
        in_specs=[pl.BlockSpec((tm, D), lambda i: (i, 0)),
                  pl.BlockSpec((1, N_MOD, D), lambda i: (i // st, 0, 0))]
        + [attn_spec(w) for w in range(n_attn)]
        + [pl.BlockSpec((tm, gw), lambda i: (i, 0)),
                  pl.BlockSpec((tm, gw), lambda i: (i, 0)),
                  _resident(ws_pairs.shape),
                  _resident((chunk, gw)),
                  _resident((1, aw)),
                  _resident((1, gw)),
                  _resident((aw + gw, D)),
                  _resident((1, D)),
                  _resident((D, 2 * d_ff)),
                  _resident((d_ff, D)),
                  _resident((1, D))],
        out_specs=pl.BlockSpec((tm, D), lambda i: (i, 0)),
        out_shape=jax.ShapeDtypeStruct((n_tok, D), F32),
        compiler_params=_cparams(1),
        name="mixffn",
    )(h, mod, *[attn] * n_attn, u, vln, ws_pairs, bias, g_attn.reshape(1, aw), g_gmlp.reshape(1, gw),
      w_out, norm_w.reshape(1, D), w_gu, w_down, norm_final.reshape(1, D))


def kernel(x, c, w_ada, b_ada, norm_ffn1, w_ffn1_gu, w_ffn1_down, norm_mix, w_in, gmlp_ln_g, gmlp_ln_b,
           gmlp_w_s, gmlp_b_s, g_attn_out, g_gmlp_out, w_out, norm_ffn2, w_ffn2_gu, w_ffn2_down, norm_final):
    B, S, D = x.shape
    depth = w_ada.shape[0]
    aw, gw = g_attn_out.shape[1], g_gmlp_out.shape[1]
    n_blk = S // MOBA_BLOCK
    assert S % (2 * MOBA_BLOCK) == 0 and aw % (ATTN_HEADS_PER_STEP * HEAD_DIM) == 0 and gw % LANES == 0
    assert n_blk <= LANES - HEAD_DIM and n_blk % 16 == 0
    tm = 2 * MOBA_BLOCK
    tm_ffn = 1024 if S % 1024 == 0 else tm
    h = x.reshape(B * S, D)
    for l in range(depth):
        last = l == depth - 1
        mod, (w_gu1, w_down1) = _ada(c, w_ada[l], b_ada[l], [w_ffn1_gu, w_ffn1_down], l)
        h, (w_in_l, w_out_l, w_gu2, w_down2) = _ffn(
            h, mod, norm_ffn1[l], w_gu1, w_down1,
            [w_in, w_out, w_ffn2_gu, w_ffn2_down], l, seq=S, mod_base=0, tm=tm_ffn)
        qt, k, vt, km, u, vln = _inproj(h, mod, norm_mix[l], w_in_l, gmlp_ln_g[l], gmlp_ln_b[l],
                                        batch=B, seq=S, aw=aw, gw=gw, tm=tm)
        attn = _moba(qt, k, vt, km, batch=B, seq=S, aw=aw)
        h = _mixffn(h, mod, attn, u, vln, gmlp_w_s[l], gmlp_b_s[l], g_attn_out[l], g_gmlp_out[l], w_out_l,
                    norm_ffn2[l], w_gu2, w_down2, norm_final, seq=S, tm=tm, final_norm=last)
    return h.reshape(B, S, D)
```

```python
import functools

import jax
import jax.numpy as jnp
import numpy as np
from jax import lax
from jax.experimental import pallas as pl
from jax.experimental.pallas import tpu as pltpu

HEAD_DIM = 64
MOBA_BLOCK = 256
MOBA_TOPK = 3
ROPE_THETA = 10000.0
NORM_EPS = 1e-6
N_MOD = 9
NEG_INF = -1e30
LANES = 128
HEADS_PER_SLAB = LANES // HEAD_DIM
ATTN_HEADS_PER_STEP = 4
ATTN_UNROLL = 14
SUM_ROWS = 16
KM_ROWS = 8
VMEM_LIMIT_BYTES = 56 * 2**20

F32 = jnp.float32
BF16 = jnp.bfloat16


def _cparams(n_axes):
    return pltpu.CompilerParams(dimension_semantics=("arbitrary",) * n_axes,
                                vmem_limit_bytes=VMEM_LIMIT_BYTES)


def _resident(shape):
    nd = len(shape)
    return pl.BlockSpec(shape, lambda *_: (0,) * nd, pipeline_mode=pl.Buffered(1))


def _rmsnorm(x, g):
    return x * lax.rsqrt(jnp.mean(x * x, axis=-1, keepdims=True) + NORM_EPS) * g


def _cast_chunk_specs(stacked_weights, layer, steps):
    in_specs, out_specs = [], []
    for w in stacked_weights:
        n = max(d for d in range(1, steps + 1) if w.shape[1] % (16 * d) == 0)
        rows = w.shape[1] // n
        in_specs.append(pl.BlockSpec((None, rows, w.shape[2]), lambda i, n=n: (layer, jnp.minimum(i, n - 1), 0)))
        out_specs.append(pl.BlockSpec((rows, w.shape[2]), lambda i, n=n: (jnp.minimum(i, n - 1), 0)))
    out_shapes = [jax.ShapeDtypeStruct(w.shape[1:], BF16) for w in stacked_weights]
    return in_specs, out_specs, out_shapes


def _cast_chunks(cast_in, cast_out):
    for src, dst in zip(cast_in, cast_out):
        dst[...] = src[...].astype(BF16)


def _ada_kernel(ct_ref, w_ref, b_ref, *rest, batch, n_casts):
    cast_in, o_ref, cast_out = rest[:n_casts], rest[n_casts], rest[n_casts + 1:]
    ca_t = jax.nn.silu(ct_ref[...])
    o_ref[...] = jnp.zeros(o_ref.shape, F32)
    for b in range(batch):
        col = jnp.broadcast_to(ca_t[:, b:b + 1], (ca_t.shape[0], LANES))
        for t in range(w_ref.shape[1] // LANES):
            lanes = slice(t * LANES, (t + 1) * LANES)
            o_ref[b:b + 1, lanes] = jnp.sum(col * w_ref[:, lanes], axis=0, keepdims=True) + b_ref[:, lanes]
    _cast_chunks(cast_in, cast_out)


def _ada(c, w_ada, b_ada, stacked_weights, layer):
    B, D = c.shape
    n_out = w_ada.shape[1]
    rows = -(-B // 8) * 8
    tn = n_out // 8 if (n_out // 8) % LANES == 0 else n_out
    steps = n_out // tn
    cast_in, cast_out, cast_shapes = _cast_chunk_specs(stacked_weights, layer, steps)
    mod, *casts = pl.pallas_call(
        functools.partial(_ada_kernel, batch=B, n_casts=len(stacked_weights)),
        grid=(steps,),
        in_specs=[pl.BlockSpec((D, B), lambda j: (0, 0)),
                  pl.BlockSpec((D, tn), lambda j: (0, j)),
                  pl.BlockSpec((1, tn), lambda j: (0, j))] + cast_in,
        out_specs=[pl.BlockSpec((rows, tn), lambda j: (0, j))] + cast_out,
        out_shape=[jax.ShapeDtypeStruct((rows, n_out), F32)] + cast_shapes,
        compiler_params=_cparams(1),
        name="ada",
    )(c.T, w_ada, b_ada.reshape(1, n_out), *stacked_weights)
    return mod[:B].reshape(B, N_MOD, D), casts


def _swiglu_halfstep(x, mod_ref, nw_ref, wgu_ref, wd_ref, *, mod_base, d_ff, tf):
    sh = mod_ref[0, mod_base:mod_base + 1, :]
    sc = mod_ref[0, mod_base + 1:mod_base + 2, :]
    gt = mod_ref[0, mod_base + 2:mod_base + 3, :]
    y = (_rmsnorm(x, nw_ref[...]) * (1.0 + sc) + sh).astype(BF16)
    acc = jnp.zeros(x.shape, F32)
    for c in range(d_ff // tf):
        g = jnp.dot(y, wgu_ref[:, c * tf:(c + 1) * tf], preferred_element_type=F32)
        u = jnp.dot(y, wgu_ref[:, d_ff + c * tf:d_ff + (c + 1) * tf], preferred_element_type=F32)
        mid = (jax.nn.silu(g) * u).astype(BF16)
        acc = acc + jnp.dot(mid, wd_ref[c * tf:(c + 1) * tf, :], preferred_element_type=F32)
    return x + (0.5 * gt) * acc


def _ffn_tile(d_ff):
    return 256 if d_ff % 256 == 0 else d_ff


def _ffn_kernel(h_ref, mod_ref, nw_ref, wgu_ref, wd_ref, *rest, mod_base, d_ff, tf, n_casts):
    cast_in, o_ref, cast_out = rest[:n_casts], rest[n_casts], rest[n_casts + 1:]
    o_ref[...] = _swiglu_halfstep(h_ref[...], mod_ref, nw_ref, wgu_ref, wd_ref, mod_base=mod_base, d_ff=d_ff, tf=tf)
    _cast_chunks(cast_in, cast_out)


def _ffn(h, mod, norm_w, w_gu, w_down, stacked_weights, layer, *, seq, mod_base, tm):
    n_tok, D = h.shape
    d_ff = w_down.shape[0]
    steps = n_tok // tm
    cast_in, cast_out, cast_shapes = _cast_chunk_specs(stacked_weights, layer, steps)
    kern = functools.partial(_ffn_kernel, mod_base=mod_base, d_ff=d_ff, tf=_ffn_tile(d_ff),
                             n_casts=len(stacked_weights))
    out, *casts = pl.pallas_call(
        kern,
        grid=(steps,),
        in_specs=[pl.BlockSpec((tm, D), lambda i: (i, 0)),
                  pl.BlockSpec((1, N_MOD, D), lambda i: (i * tm // seq, 0, 0)),
                  _resident((1, D)),
                  _resident((D, 2 * d_ff)),
                  _resident((d_ff, D))] + cast_in,
        out_specs=[pl.BlockSpec((tm, D), lambda i: (i, 0))] + cast_out,
        out_shape=[jax.ShapeDtypeStruct((n_tok, D), F32)] + cast_shapes,
        compiler_params=_cparams(1),
        name="ffn",
    )(h, mod, norm_w.reshape(1, D), w_gu, w_down, *stacked_weights)
    return out, casts


def _rope(x, cos, sin_signed, first_half):
    partner = jnp.where(first_half, pltpu.roll(x, LANES - HEAD_DIM // 2, 1), pltpu.roll(x, HEAD_DIM // 2, 1))
    return x * cos + partner * sin_signed


def _inproj_kernel(h_ref, mod_ref, nw_ref, win_ref, cos_ref, sin_ref, lng_ref, lnb_ref,
                   qt_ref, k_ref, vt_ref, km_ref, u_ref, vln_ref, *, aw, gw, st):
    x = h_ref[...]
    sh = mod_ref[0, 3:4, :]
    sc = mod_ref[0, 4:5, :]
    y = (_rmsnorm(x, nw_ref[...]) * (1.0 + sc) + sh).astype(BF16)

    proj = jnp.dot(y, win_ref[...], preferred_element_type=F32)
    cos = cos_ref[...]
    sin_signed = sin_ref[...]
    tm = x.shape[0]
    bpt = tm // MOBA_BLOCK
    lane = lax.broadcasted_iota(jnp.int32, cos.shape, 1)
    first_half = (lane % HEAD_DIM) < (HEAD_DIM // 2)
    is_key_lane = lane < HEAD_DIM
    row = lax.broadcasted_iota(jnp.int32, cos.shape, 0)
    blk_of_row = (pl.program_id(0) % st) * bpt + row // MOBA_BLOCK
    blk_onehot = jnp.where(lane - HEAD_DIM == blk_of_row, 1.0, 0.0)
    scale = HEAD_DIM ** -0.5 * np.log2(np.e)
    km_ref[...] = jnp.zeros(km_ref.shape, F32)
    for s in range(aw // LANES):
        qs = _rope(proj[:, s * LANES:(s + 1) * LANES], cos, sin_signed, first_half) * scale
        ks = _rope(proj[:, aw + s * LANES:aw + (s + 1) * LANES], cos, sin_signed, first_half)
        k_ref[2 * s] = jnp.where(is_key_lane, ks, blk_onehot).astype(BF16)
        k_ref[2 * s + 1] = jnp.where(is_key_lane, pltpu.roll(ks, HEAD_DIM, 1), blk_onehot).astype(BF16)
        vs = proj[:, 2 * aw + s * LANES:2 * aw + (s + 1) * LANES]
        for sb in range(bpt):
            rows = slice(sb * MOBA_BLOCK, (sb + 1) * MOBA_BLOCK)
            km_ref[sb:sb + 1, s * LANES:(s + 1) * LANES] = jnp.mean(ks[rows], axis=0, keepdims=True)
            qt_ref[sb, s * LANES:(s + 1) * LANES, :] = qs[rows].T.astype(BF16)
            vt_ref[sb, s * LANES:(s + 1) * LANES, :] = vs[rows].T.astype(BF16)
    u_ref[...] = jax.nn.gelu(proj[:, 3 * aw:3 * aw + gw]).astype(BF16)
    gv = jax.nn.gelu(proj[:, 3 * aw + gw:3 * aw + 2 * gw])
    mu = jnp.mean(gv, axis=-1, keepdims=True)
    var = jnp.mean(jnp.square(gv - mu), axis=-1, keepdims=True)
    vln = (gv - mu) * lax.rsqrt(var + NORM_EPS) * lng_ref[...] + lnb_ref[...]
    vln_ref[...] = vln.astype(BF16)


def _rope_tables(seq):
    half = HEAD_DIM // 2
    pos = jnp.arange(seq, dtype=F32)
    inv_freq = ROPE_THETA ** (-jnp.arange(half, dtype=F32) / half)
    ang = pos[:, None] * inv_freq[None, :]
    cos, sin = jnp.cos(ang), jnp.sin(ang)
    return (jnp.tile(jnp.concatenate([cos, cos], axis=-1), (1, HEADS_PER_SLAB)),
            jnp.tile(jnp.concatenate([-sin, sin], axis=-1), (1, HEADS_PER_SLAB)))


def _inproj(h, mod, norm_w, w_in, ln_g, ln_b, *, batch, seq, aw, gw, tm):
    n_tok, D = h.shape
    n_heads = aw // HEAD_DIM
    st = seq // tm
    bpt = tm // MOBA_BLOCK
    n_blk = seq // MOBA_BLOCK
    cos_t, sin_t = _rope_tables(seq)
    kern = functools.partial(_inproj_kernel, aw=aw, gw=gw, st=st)
    qt, k, vt, km, u, vln = pl.pallas_call(
        kern,
        grid=(n_tok // tm,),
        in_specs=[pl.BlockSpec((tm, D), lambda i: (i, 0)),
                  pl.BlockSpec((1, N_MOD, D), lambda i: (i // st, 0, 0)),
                  _resident((1, D)),
                  _resident((D, 3 * aw + 2 * gw)),
                  pl.BlockSpec((tm, LANES), lambda i: (i % st, 0)),
                  pl.BlockSpec((tm, LANES), lambda i: (i % st, 0)),
                  _resident((1, gw)),
                  _resident((1, gw))],
        out_specs=[pl.BlockSpec((None, bpt, aw, MOBA_BLOCK), lambda i: (i // st, i % st, 0, 0)),
                   pl.BlockSpec((None, n_heads, tm, LANES), lambda i: (i // st, 0, i % st, 0)),
                   pl.BlockSpec((None, bpt, aw, MOBA_BLOCK), lambda i: (i // st, i % st, 0, 0)),
                   pl.BlockSpec((None, None, KM_ROWS, aw), lambda i: (i // st, i % st, 0, 0)),
                   pl.BlockSpec((tm, gw), lambda i: (i, 0)),
                   pl.BlockSpec((tm, gw), lambda i: (i, 0))],
        out_shape=[jax.ShapeDtypeStruct((batch, n_blk, aw, MOBA_BLOCK), BF16),
                   jax.ShapeDtypeStruct((batch, n_heads, seq, LANES), BF16),
                   jax.ShapeDtypeStruct((batch, n_blk, aw, MOBA_BLOCK), BF16),
                   jax.ShapeDtypeStruct((batch, st, KM_ROWS, aw), F32),
                   jax.ShapeDtypeStruct((n_tok, gw), BF16),
                   jax.ShapeDtypeStruct((n_tok, gw), BF16)],
        compiler_params=_cparams(1),
        name="inproj",
    )(h, mod, norm_w.reshape(1, D), w_in, cos_t, sin_t, ln_g.reshape(1, gw), ln_b.reshape(1, gw))
    km = km[:, :, :bpt, :].reshape(batch, n_blk, aw)
    return qt, k, vt, km, u, vln


def _moba_kernel(qa_ref, qb_ref, qt_all_ref, km_ref, k_ref, vt_ref, o_ref, qtail_ref, qaug_ref, s_ref, cmax_ref,
                 *state_refs, n_blk):
    p = pl.program_id(2)
    blk = MOBA_BLOCK
    n_h = ATTN_HEADS_PER_STEP
    heads = range(n_h)
    m_refs, acc_refs = state_refs[:n_h], state_refs[n_h:]
    zeros_half = jnp.zeros((HEAD_DIM, blk), BF16)
    zeros_tail = jnp.zeros((LANES - HEAD_DIM - n_blk, blk), BF16)

    blk_a, blk_b = p, n_blk - 1 - p
    blk_id = lax.broadcasted_iota(jnp.int32, (n_blk, blk), 0)

    def gate_blocks(blocks):
        km_pieces = []
        for slab in range(n_h // HEADS_PER_SLAB):
            rest = km_ref[:, slab * LANES:(slab + 1) * LANES]
            pieces = []
            for _ in range(3):
                piece = rest.astype(BF16)
                rest = rest - piece.astype(F32)
                pieces.append(piece)
            km_pieces.append(pieces)
        for ib in blocks:
            for hh in heads:
                qt = qt_all_ref[ib, hh * HEAD_DIM:(hh + 1) * HEAD_DIM, :]
                halves = [zeros_half, qt] if hh % HEADS_PER_SLAB else [qt, zeros_half]
                q_slab = jnp.concatenate(halves, axis=0)
                gate = sum(jnp.dot(piece, q_slab, preferred_element_type=F32)
                           for piece in reversed(km_pieces[hh // HEADS_PER_SLAB]))
                gate = jnp.where(blk_id < ib, gate, NEG_INF)
                bias = jnp.where(blk_id == ib, 0.0, NEG_INF)
                for r in range(min(MOBA_TOPK, n_blk)):
                    top = jnp.max(gate, axis=0, keepdims=True)
                    idx = jnp.min(jnp.where(gate == top, blk_id, n_blk), axis=0, keepdims=True)
                    hit = blk_id == idx
                    rank_bias = jnp.where(r < ib, 0.0, NEG_INF)
                    bias = jnp.where(hit, jnp.maximum(bias, rank_bias), bias)
                    gate = jnp.where(hit, -jnp.inf, gate)
                qtail_ref[ib, hh] = bias.astype(BF16)

    @pl.when(p == 0)
    def _():
        gate_blocks([blk_a, blk_b])
    ones_rows = jnp.ones((SUM_ROWS, blk), BF16)
    key_pos = lax.broadcasted_iota(jnp.int32, (blk, blk), 0)
    q_pos = lax.broadcasted_iota(jnp.int32, (blk, blk), 1)
    causal = key_pos <= q_pos

    for slot, (q_ref, own) in enumerate(((qb_ref, blk_b), (qa_ref, blk_a))):
        for hh in heads:
            qaug_ref[slot, hh] = jnp.concatenate(
                [q_ref[hh * HEAD_DIM:(hh + 1) * HEAD_DIM, :], qtail_ref[own, hh]], axis=0)
            m_refs[hh][slot] = jnp.full((1, blk), NEG_INF, F32)
            acc_refs[hh][slot] = jnp.zeros((HEAD_DIM + SUM_ROWS, blk), F32)

    def unit(t):
        n = t - 2
        slot = jnp.where(t < 2, t, jnp.where(n >= blk_b, 1, 0))
        j = jnp.where(t == 0, blk_b, jnp.where(t == 1, blk_a, jnp.where(n >= blk_b, n - blk_b, n)))
        return slot, j

    def score(hh, slot, j, buf, own):
        start = pl.multiple_of(j * blk, blk)
        q_aug = jnp.concatenate([qaug_ref[slot, hh], zeros_tail], axis=0)
        s = jnp.dot(k_ref[hh, pl.ds(start, blk), :], q_aug, preferred_element_type=F32)
        if own:
            s = jnp.where(causal, s, NEG_INF)
        s_ref[buf, hh] = s
        cmax_ref[buf, hh] = jnp.max(s, axis=0, keepdims=True)

    def accumulate(hh, slot, j, buf):
        m_old = m_refs[hh][slot]
        m_new = jnp.maximum(m_old, cmax_ref[buf, hh])
        e = jnp.exp2(s_ref[buf, hh] - m_new).astype(BF16)
        vt = jnp.concatenate([vt_ref[j, hh * HEAD_DIM:(hh + 1) * HEAD_DIM, :], ones_rows], axis=0)
        pv = jnp.dot(vt, e, preferred_element_type=F32)
        m_refs[hh][slot] = m_new
        acc_refs[hh][slot] = jnp.exp2(m_old - m_new) * acc_refs[hh][slot] + pv

    n_units = n_blk + 1
    for t in range(2):
        for hh in heads:
            score(hh, *unit(t), t, True)

    next_a, next_b = jnp.minimum(blk_a + 1, n_blk // 2 - 1), jnp.maximum(blk_b - 1, n_blk // 2)

    def stages(t, buf):
        now, ahead = unit(t), unit(t + 2)
        for hh in heads:
            accumulate(hh, *now, buf)
            score(hh, *ahead, buf, False)

    n_loop = (n_units - 2) // ATTN_UNROLL * ATTN_UNROLL

    def body(it, carry):
        gate_blocks([jnp.where(it == 0, next_a, next_b)] if n_iter > 1 else [next_a, next_b])
        for k in range(ATTN_UNROLL):
            stages(it * ATTN_UNROLL + k, k % 2)
        return carry

    n_iter = n_loop // ATTN_UNROLL
    lax.fori_loop(0, n_iter, body, 0)
    for t in range(n_loop, n_units - 2):
        stages(t, t % 2)
    for t in range(n_units - 2, n_units):
        for hh in heads:
            accumulate(hh, *unit(t), t % 2)

    for slot in range(2):
        outs = []
        for hh in heads:
            acc = acc_refs[hh][slot]
            outs.append(acc[:HEAD_DIM] * (1.0 / acc[HEAD_DIM:HEAD_DIM + 1]))
        o_ref[1 - slot] = jnp.concatenate(outs, axis=0).T.astype(BF16)


def _moba(qt, k, vt, km, *, batch, seq, aw):
    n_blk = seq // MOBA_BLOCK
    n_h = ATTN_HEADS_PER_STEP
    width = n_h * HEAD_DIM
    kern = functools.partial(_moba_kernel, n_blk=n_blk)
    return pl.pallas_call(
        kern,
        grid=(batch, aw // width, n_blk // 2),
        in_specs=[pl.BlockSpec((None, None, width, MOBA_BLOCK), lambda b, hg, p: (b, p, hg, 0)),
                  pl.BlockSpec((None, None, width, MOBA_BLOCK), lambda b, hg, p: (b, n_blk - 1 - p, hg, 0)),
                  pl.BlockSpec((None, n_blk, width, MOBA_BLOCK), lambda b, hg, p: (b, 0, hg, 0)),
                  pl.BlockSpec((None, n_blk, width), lambda b, hg, p: (b, 0, hg)),
                  pl.BlockSpec((None, n_h, seq, LANES), lambda b, hg, p: (b, hg, 0, 0)),
                  pl.BlockSpec((None, n_blk, width, MOBA_BLOCK), lambda b, hg, p: (b, 0, hg, 0))],
        out_specs=pl.BlockSpec((None, 2, None, MOBA_BLOCK, width), lambda b, hg, p: (b, 0, p, 0, hg)),
        out_shape=jax.ShapeDtypeStruct((batch, 2, n_blk // 2, MOBA_BLOCK, aw), BF16),
        scratch_shapes=[pltpu.VMEM((n_blk, n_h, n_blk, MOBA_BLOCK), BF16),
                        pltpu.VMEM((2, n_h, HEAD_DIM + n_blk, MOBA_BLOCK), BF16),
                        pltpu.VMEM((2, n_h, MOBA_BLOCK, MOBA_BLOCK), F32),
                        pltpu.VMEM((2, n_h, 1, MOBA_BLOCK), F32)]
        + [pltpu.VMEM((2, 1, MOBA_BLOCK), F32)] * n_h
        + [pltpu.VMEM((2, HEAD_DIM + SUM_ROWS, MOBA_BLOCK), F32)] * n_h,
        compiler_params=_cparams(3),
        name="moba",
    )(qt, qt, qt, km, k, vt)


def _mixffn_kernel(h_ref, mod_ref, *refs, n_attn, aw, gw, chunk, d_ff, tf, final_norm):
    attn_refs = refs[:n_attn]
    u_ref, vln_ref, ws_ref, bias_ref, ga_ref, gg_ref, wout_ref, nw_ref, wgu_ref, wd_ref, nf_ref, o_ref = refs[n_attn:]
    tm = h_ref.shape[0]
    gt = mod_ref[0, 5:6, :]
    attn = jnp.concatenate([a[...] for a in attn_refs], axis=0).astype(F32)
    attn_n = _rmsnorm(attn, ga_ref[...]).astype(BF16)

    n_grp = gw // LANES
    row = lax.broadcasted_iota(jnp.int32, (HEADS_PER_SLAB * chunk, chunk), 0)
    col = lax.broadcasted_iota(jnp.int32, (HEADS_PER_SLAB * chunk, chunk), 1)
    keep = (row % chunk) >= col
    w_pairs = [jnp.where(keep, ws_ref[g], 0.0).astype(BF16) for g in range(n_grp)]
    lane = lax.broadcasted_iota(jnp.int32, (chunk, LANES), 1)
    first_head = lane < HEAD_DIM
    bias = bias_ref[...]

    rows = []
    for c in range(tm // chunk):
        vc = vln_ref[c * chunk:(c + 1) * chunk, :]
        mixed = []
        for g in range(n_grp):
            r = jnp.dot(w_pairs[g], vc[:, g * LANES:(g + 1) * LANES], preferred_element_type=F32)
            mixed.append(jnp.where(first_head, r[:chunk], r[chunk:]))
        mixed = jnp.concatenate(mixed, axis=1) + bias
        rows.append(u_ref[c * chunk:(c + 1) * chunk, :].astype(F32) * mixed)
    gm = jnp.concatenate(rows, axis=0)
    gmlp_n = _rmsnorm(gm, gg_ref[...]).astype(BF16)

    y = jnp.dot(attn_n, wout_ref[:aw, :], preferred_element_type=F32)
    y = y + jnp.dot(gmlp_n, wout_ref[aw:, :], preferred_element_type=F32)
    h2 = h_ref[...] + gt * y
    out = _swiglu_halfstep(h2, mod_ref, nw_ref, wgu_ref, wd_ref, mod_base=6, d_ff=d_ff, tf=tf)
    if final_norm:
        out = _rmsnorm(out, nf_ref[...])
    o_ref[...] = out


def _mixffn(h, mod, attn, u, vln, w_s, b_s, g_attn, g_gmlp, w_out, norm_w, w_gu, w_down, norm_final,
            *, seq, tm, final_norm):
    n_tok, D = h.shape
    aw, gw = g_attn.shape[0], g_gmlp.shape[0]
    n_gh, chunk = w_s.shape[0], w_s.shape[1]
    d_ff = w_down.shape[0]
    st = seq // tm
    n_attn = tm // MOBA_BLOCK
    half = seq // MOBA_BLOCK // 2
    ws_pairs = w_s.reshape(n_gh // HEADS_PER_SLAB, HEADS_PER_SLAB * chunk, chunk)
    bias = jnp.repeat(b_s.T, gw // n_gh, axis=1)

    def attn_spec(which):
        def index(i):
            j = (i % st) * n_attn + which
            return (i // st, j // half, jnp.where(j < half, j, 2 * half - 1 - j), 0, 0)
        return pl.BlockSpec((None, None, None, MOBA_BLOCK, aw), index)

    kern = functools.partial(_mixffn_kernel, n_attn=n_attn, aw=aw, gw=gw, chunk=chunk, d_ff=d_ff,
                             tf=_ffn_tile(d_ff), final_norm=final_norm)
    return pl.pallas_call(
        kern,
        grid=(n_tok // tm,),
        in_specs=[pl.BlockSpec((tm, D), lambda i: (i, 0)),
                  pl.BlockSpec((1, N_MOD, D), lambda i: (i // st, 0, 0))]
        + [attn_spec(w) for w in range(n_attn)]
        + [pl.BlockSpec((tm, gw), lambda i: (i, 0)),
                  pl.BlockSpec((tm, gw), lambda i: (i, 0)),
                  _resident(ws_pairs.shape),
                  _resident((chunk, gw)),
                  _resident((1, aw)),
                  _resident((1, gw)),
                  _resident((aw + gw, D)),
                  _resident((1, D)),
                  _resident((D, 2 * d_ff)),
                  _resident((d_ff, D)),
                  _resident((1, D))],
        out_specs=pl.BlockSpec((tm, D), lambda i: (i, 0)),
        out_shape=jax.ShapeDtypeStruct((n_tok, D), F32),
        compiler_params=_cparams(1),
        name="mixffn",
    )(h, mod, *[attn] * n_attn, u, vln, ws_pairs, bias, g_attn.reshape(1, aw), g_gmlp.reshape(1, gw),
      w_out, norm_w.reshape(1, D), w_gu, w_down, norm_final.reshape(1, D))


def kernel(x, c, w_ada, b_ada, norm_ffn1, w_ffn1_gu, w_ffn1_down, norm_mix, w_in, gmlp_ln_g, gmlp_ln_b,
           gmlp_w_s, gmlp_b_s, g_attn_out, g_gmlp_out, w_out, norm_ffn2, w_ffn2_gu, w_ffn2_down, norm_final):
    B, S, D = x.shape
    depth = w_ada.shape[0]
    aw, gw = g_attn_out.shape[1], g_gmlp_out.shape[1]
    n_blk = S // MOBA_BLOCK
    assert S % (2 * MOBA_BLOCK) == 0 and aw % (ATTN_HEADS_PER_STEP * HEAD_DIM) == 0 and gw % LANES == 0
    assert n_blk <= LANES - HEAD_DIM and n_blk % 16 == 0
    tm = 2 * MOBA_BLOCK
    tm_ffn = 1024 if S % 1024 == 0 else tm
    h = x.reshape(B * S, D)
    for l in range(depth):
        last = l == depth - 1
        mod, (w_gu1, w_down1) = _ada(c, w_ada[l], b_ada[l], [w_ffn1_gu, w_ffn1_down], l)
        h, (w_in_l, w_out_l, w_gu2, w_down2) = _ffn(
            h, mod, norm_ffn1[l], w_gu1, w_down1,
            [w_in, w_out, w_ffn2_gu, w_ffn2_down], l, seq=S, mod_base=0, tm=tm_ffn)
        qt, k, vt, km, u, vln = _inproj(h, mod, norm_mix[l], w_in_l, gmlp_ln_g[l], gmlp_ln_b[l],
                                        batch=B, seq=S, aw=aw, gw=gw, tm=tm)
        attn = _moba(qt, k, vt, km, batch=B, seq=S, aw=aw)
        h = _mixffn(h, mod, attn, u, vln, gmlp_w_s[l], gmlp_b_s[l], g_attn_out[l], g_gmlp_out[l], w_out_l,
                    norm_ffn2[l], w_gu2, w_down2, norm_final, seq=S, tm=tm, final_norm=last)
    return h.reshape(B, S, D)
```

```python
import functools

import jax
import jax.numpy as jnp
import numpy as np
from jax import lax
from jax.experimental import pallas as pl
from jax.experimental.pallas import tpu as pltpu

HEAD_DIM = 64
MOBA_BLOCK = 256
MOBA_TOPK = 3
ROPE_THETA = 10000.0
NORM_EPS = 1e-6
N_MOD = 9
NEG_INF = -1e30
LANES = 128
HEADS_PER_SLAB = LANES // HEAD_DIM
ATTN_HEADS_PER_STEP = 4
ATTN_UNROLL = 14
SUM_ROWS = 16
KM_ROWS = 8
VMEM_LIMIT_BYTES = 56 * 2**20

F32 = jnp.float32
BF16 = jnp.bfloat16


def _cparams(n_axes):
    return pltpu.CompilerParams(dimension_semantics=("arbitrary",) * n_axes,
                                vmem_limit_bytes=VMEM_LIMIT_BYTES)


def _resident(shape):
    nd = len(shape)
    return pl.BlockSpec(shape, lambda *_: (0,) * nd, pipeline_mode=pl.Buffered(1))


def _rmsnorm(x, g):
    return x * lax.rsqrt(jnp.mean(x * x, axis=-1, keepdims=True) + NORM_EPS) * g


def _cast_chunk_specs(stacked_weights, layer, steps):
    in_specs, out_specs = [], []
    for w in stacked_weights:
        n = max(d for d in range(1, steps + 1) if w.shape[1] % (16 * d) == 0)
        rows = w.shape[1] // n
        in_specs.append(pl.BlockSpec((None, rows, w.shape[2]), lambda i, n=n: (layer, jnp.minimum(i, n - 1), 0)))
        out_specs.append(pl.BlockSpec((rows, w.shape[2]), lambda i, n=n: (jnp.minimum(i, n - 1), 0)))
    out_shapes = [jax.ShapeDtypeStruct(w.shape[1:], BF16) for w in stacked_weights]
    return in_specs, out_specs, out_shapes


def _cast_chunks(cast_in, cast_out):
    for src, dst in zip(cast_in, cast_out):
        dst[...] = src[...].astype(BF16)


def _ada_kernel(ct_ref, w_ref, b_ref, *rest, batch, n_casts):
    cast_in, o_ref, cast_out = rest[:n_casts], rest[n_casts], rest[n_casts + 1:]
    ca_t = jax.nn.silu(ct_ref[...])
    o_ref[...] = jnp.zeros(o_ref.shape, F32)
    for b in range(batch):
        col = jnp.broadcast_to(ca_t[:, b:b + 1], (ca_t.shape[0], LANES))
        for t in range(w_ref.shape[1] // LANES):
            lanes = slice(t * LANES, (t + 1) * LANES)
            o_ref[b:b + 1, lanes] = jnp.sum(col * w_ref[:, lanes], axis=0, keepdims=True) + b_ref[:, lanes]
    _cast_chunks(cast_in, cast_out)


def _ada(c, w_ada, b_ada, stacked_weights, layer):
    B, D = c.shape
    n_out = w_ada.shape[1]
    rows = -(-B // 8) * 8
    tn = n_out // 8 if (n_out // 8) % LANES == 0 else n_out
    steps = n_out // tn
    cast_in, cast_out, cast_shapes = _cast_chunk_specs(stacked_weights, layer, steps)
    mod, *casts = pl.pallas_call(
        functools.partial(_ada_kernel, batch=B, n_casts=len(stacked_weights)),
        grid=(steps,),
        in_specs=[pl.BlockSpec((D, B), lambda j: (0, 0)),
                  pl.BlockSpec((D, tn), lambda j: (0, j)),
                  pl.BlockSpec((1, tn), lambda j: (0, j))] + cast_in,
        out_specs=[pl.BlockSpec((rows, tn), lambda j: (0, j))] + cast_out,
        out_shape=[jax.ShapeDtypeStruct((rows, n_out), F32)] + cast_shapes,
        compiler_params=_cparams(1),
        name="ada",
    )(c.T, w_ada, b_ada.reshape(1, n_out), *stacked_weights)
    return mod[:B].reshape(B, N_MOD, D), casts


def _swiglu_halfstep(x, mod_ref, nw_ref, wgu_ref, wd_ref, *, mod_base, d_ff, tf):
    sh = mod_ref[0, mod_base:mod_base + 1, :]
    sc = mod_ref[0, mod_base + 1:mod_base + 2, :]
    gt = mod_ref[0, mod_base + 2:mod_base + 3, :]
    y = (_rmsnorm(x, nw_ref[...]) * (1.0 + sc) + sh).astype(BF16)
    acc = jnp.zeros(x.shape, F32)
    for c in range(d_ff // tf):
        g = jnp.dot(y, wgu_ref[:, c * tf:(c + 1) * tf], preferred_element_type=F32)
        u = jnp.dot(y, wgu_ref[:, d_ff + c * tf:d_ff + (c + 1) * tf], preferred_element_type=F32)
        mid = (jax.nn.silu(g) * u).astype(BF16)
        acc = acc + jnp.dot(mid, wd_ref[c * tf:(c + 1) * tf, :], preferred_element_type=F32)
    return x + (0.5 * gt) * acc


def _ffn_tile(d_ff):
    return 256 if d_ff % 256 == 0 else d_ff


def _ffn_kernel(h_ref, mod_ref, nw_ref, wgu_ref, wd_ref, *rest, mod_base, d_ff, tf, n_casts):
    cast_in, o_ref, cast_out = rest[:n_casts], rest[n_casts], rest[n_casts + 1:]
    o_ref[...] = _swiglu_halfstep(h_ref[...], mod_ref, nw_ref, wgu_ref, wd_ref, mod_base=mod_base, d_ff=d_ff, tf=tf)
    _cast_chunks(cast_in, cast_out)


def _ffn(h, mod, norm_w, w_gu, w_down, stacked_weights, layer, *, seq, mod_base, tm):
    n_tok, D = h.shape
    d_ff = w_down.shape[0]
    steps = n_tok // tm
    cast_in, cast_out, cast_shapes = _cast_chunk_specs(stacked_weights, layer, steps)
    kern = functools.partial(_ffn_kernel, mod_base=mod_base, d_ff=d_ff, tf=_ffn_tile(d_ff),
                             n_casts=len(stacked_weights))
    out, *casts = pl.pallas_call(
        kern,
        grid=(steps,),
        in_specs=[pl.BlockSpec((tm, D), lambda i: (i, 0)),
                  pl.BlockSpec((1, N_MOD, D), lambda i: (i * tm // seq, 0, 0)),
                  _resident((1, D)),
                  _resident((D, 2 * d_ff)),
                  _resident((d_ff, D))] + cast_in,
        out_specs=[pl.BlockSpec((tm, D), lambda i: (i, 0))] + cast_out,
        out_shape=[jax.ShapeDtypeStruct((n_tok, D), F32)] + cast_shapes,
        compiler_params=_cparams(1),
        name="ffn",
    )(h, mod, norm_w.reshape(1, D), w_gu, w_down, *stacked_weights)
    return out, casts


def _rope(x, cos, sin_signed, first_half):
    partner = jnp.where(first_half, pltpu.roll(x, LANES - HEAD_DIM // 2, 1), pltpu.roll(x, HEAD_DIM // 2, 1))
    return x * cos + partner * sin_signed


def _inproj_kernel(h_ref, mod_ref, nw_ref, win_ref, cos_ref, sin_ref, lng_ref, lnb_ref,
                   qt_ref, k_ref, vt_ref, km_ref, u_ref, vln_ref, *, aw, gw, st):
    x = h_ref[...]
    sh = mod_ref[0, 3:4, :]
    sc = mod_ref[0, 4:5, :]
    y = (_rmsnorm(x, nw_ref[...]) * (1.0 + sc) + sh).astype(BF16)

    proj = jnp.dot(y, win_ref[...], preferred_element_type=F32)
    cos = cos_ref[...]
    sin_signed = sin_ref[...]
    tm = x.shape[0]
    bpt = tm // MOBA_BLOCK
    lane = lax.broadcasted_iota(jnp.int32, cos.shape, 1)
    first_half = (lane % HEAD_DIM) < (HEAD_DIM // 2)
    is_key_lane = lane < HEAD_DIM
    row = lax.broadcasted_iota(jnp.int32, cos.shape, 0)
    blk_of_row = (pl.program_id(0) % st) * bpt + row // MOBA_BLOCK
    blk_onehot = jnp.where(lane - HEAD_DIM == blk_of_row, 1.0, 0.0)
    scale = HEAD_DIM ** -0.5 * np.log2(np.e)
    km_ref[...] = jnp.zeros(km_ref.shape, F32)
    for s in range(aw // LANES):
        qs = _rope(proj[:, s * LANES:(s + 1) * LANES], cos, sin_signed, first_half) * scale
        ks = _rope(proj[:, aw + s * LANES:aw + (s + 1) * LANES], cos, sin_signed, first_half)
        k_ref[2 * s] = jnp.where(is_key_lane, ks, blk_onehot).astype(BF16)
        k_ref[2 * s + 1] = jnp.where(is_key_lane, pltpu.roll(ks, HEAD_DIM, 1), blk_onehot).astype(BF16)
        vs = proj[:, 2 * aw + s * LANES:2 * aw + (s + 1) * LANES]
        for sb in range(bpt):
            rows = slice(sb * MOBA_BLOCK, (sb + 1) * MOBA_BLOCK)
            km_ref[sb:sb + 1, s * LANES:(s + 1) * LANES] = jnp.mean(ks[rows], axis=0, keepdims=True)
            qt_ref[sb, s * LANES:(s + 1) * LANES, :] = qs[rows].T.astype(BF16)
            vt_ref[sb, s * LANES:(s + 1) * LANES, :] = vs[rows].T.astype(BF16)
    u_ref[...] = jax.nn.gelu(proj[:, 3 * aw:3 * aw + gw]).astype(BF16)
    gv = jax.nn.gelu(proj[:, 3 * aw + gw:3 * aw + 2 * gw])
    mu = jnp.mean(gv, axis=-1, keepdims=True)
    var = jnp.mean(jnp.square(gv - mu), axis=-1, keepdims=True)
    vln = (gv - mu) * lax.rsqrt(var + NORM_EPS) * lng_ref[...] + lnb_ref[...]
    vln_ref[...] = vln.astype(BF16)


def _rope_tables(seq):
    half = HEAD_DIM // 2
    pos = jnp.arange(seq, dtype=F32)
    inv_freq = ROPE_THETA ** (-jnp.arange(half, dtype=F32) / half)
    inv_freq_lane = jnp.tile(inv_freq, LANES // half)
    sign_lane = jnp.tile(jnp.concatenate([-jnp.ones(half, F32), jnp.ones(half, F32)]), HEADS_PER_SLAB)
    ang = pos[:, None] * inv_freq_lane[None, :]
    return jnp.cos(ang), jnp.sin(ang) * sign_lane[None, :]


def _inproj(h, mod, norm_w, w_in, ln_g, ln_b, *, batch, seq, aw, gw, tm):
    n_tok, D = h.shape
    n_heads = aw // HEAD_DIM
    st = seq // tm
    bpt = tm // MOBA_BLOCK
    n_blk = seq // MOBA_BLOCK
    cos_t, sin_t = _rope_tables(seq)
    kern = functools.partial(_inproj_kernel, aw=aw, gw=gw, st=st)
    qt, k, vt, km, u, vln = pl.pallas_call(
        kern,
        grid=(n_tok // tm,),
        in_specs=[pl.BlockSpec((tm, D), lambda i: (i, 0)),
                  pl.BlockSpec((1, N_MOD, D), lambda i: (i // st, 0, 0)),
                  _resident((1, D)),
                  _resident((D, 3 * aw + 2 * gw)),
                  pl.BlockSpec((tm, LANES), lambda i: (i % st, 0)),
                  pl.BlockSpec((tm, LANES), lambda i: (i % st, 0)),
                  _resident((1, gw)),
                  _resident((1, gw))],
        out_specs=[pl.BlockSpec((None, bpt, aw, MOBA_BLOCK), lambda i: (i // st, i % st, 0, 0)),
                   pl.BlockSpec((None, n_heads, tm, LANES), lambda i: (i // st, 0, i % st, 0)),
                   pl.BlockSpec((None, bpt, aw, MOBA_BLOCK), lambda i: (i // st, i % st, 0, 0)),
                   pl.BlockSpec((None, None, KM_ROWS, aw), lambda i: (i // st, i % st, 0, 0)),
                   pl.BlockSpec((tm, gw), lambda i: (i, 0)),
                   pl.BlockSpec((tm, gw), lambda i: (i, 0))],
        out_shape=[jax.ShapeDtypeStruct((batch, n_blk, aw, MOBA_BLOCK), BF16),
                   jax.ShapeDtypeStruct((batch, n_heads, seq, LANES), BF16),
                   jax.ShapeDtypeStruct((batch, n_blk, aw, MOBA_BLOCK), BF16),
                   jax.ShapeDtypeStruct((batch, st, KM_ROWS, aw), F32),
                   jax.ShapeDtypeStruct((n_tok, gw), BF16),
                   jax.ShapeDtypeStruct((n_tok, gw), BF16)],
        compiler_params=_cparams(1),
        name="inproj",
    )(h, mod, norm_w.reshape(1, D), w_in, cos_t, sin_t, ln_g.reshape(1, gw), ln_b.reshape(1, gw))
    km = km[:, :, :bpt, :].reshape(batch, n_blk, aw)
    return qt, k, vt, km, u, vln


def _moba_kernel(qa_ref, qb_ref, qt_all_ref, km_ref, k_ref, vt_ref, o_ref, qtail_ref, qaug_ref, s_ref, cmax_ref,
                 *state_refs, n_blk):
    p = pl.program_id(2)
    blk = MOBA_BLOCK
    n_h = ATTN_HEADS_PER_STEP
    heads = range(n_h)
    m_refs, acc_refs = state_refs[:n_h], state_refs[n_h:]
    zeros_half = jnp.zeros((HEAD_DIM, blk), BF16)
    zeros_tail = jnp.zeros((LANES - HEAD_DIM - n_blk, blk), BF16)

    blk_a, blk_b = p, n_blk - 1 - p
    blk_id = lax.broadcasted_iota(jnp.int32, (n_blk, blk), 0)

    def gate_blocks(blocks):
        km_pieces = []
        for slab in range(n_h // HEADS_PER_SLAB):
            rest = km_ref[:, slab * LANES:(slab + 1) * LANES]
            pieces = []
            for _ in range(3):
                piece = rest.astype(BF16)
                rest = rest - piece.astype(F32)
                pieces.append(piece)
            km_pieces.append(pieces)
        for ib in blocks:
            for hh in heads:
                qt = qt_all_ref[ib, hh * HEAD_DIM:(hh + 1) * HEAD_DIM, :]
                halves = [zeros_half, qt] if hh % HEADS_PER_SLAB else [qt, zeros_half]
                q_slab = jnp.concatenate(halves, axis=0)
                gate = sum(jnp.dot(piece, q_slab, preferred_element_type=F32)
                           for piece in reversed(km_pieces[hh // HEADS_PER_SLAB]))
                gate = jnp.where(blk_id < ib, gate, NEG_INF)
                bias = jnp.where(blk_id == ib, 0.0, NEG_INF)
                for r in range(min(MOBA_TOPK, n_blk)):
                    top = jnp.max(gate, axis=0, keepdims=True)
                    idx = jnp.min(jnp.where(gate == top, blk_id, n_blk), axis=0, keepdims=True)
                    hit = blk_id == idx
                    rank_bias = jnp.where(r < ib, 0.0, NEG_INF)
                    bias = jnp.where(hit, jnp.maximum(bias, rank_bias), bias)
                    gate = jnp.where(hit, -jnp.inf, gate)
                qtail_ref[ib, hh] = bias.astype(BF16)

    @pl.when(p == 0)
    def _():
        gate_blocks([blk_a, blk_b])
    ones_rows = jnp.ones((SUM_ROWS, blk), BF16)
    key_pos = lax.broadcasted_iota(jnp.int32, (blk, blk), 0)
    q_pos = lax.broadcasted_iota(jnp.int32, (blk, blk), 1)
    causal = key_pos <= q_pos

    for slot, (q_ref, own) in enumerate(((qb_ref, blk_b), (qa_ref, blk_a))):
        for hh in heads:
            qaug_ref[slot, hh] = jnp.concatenate(
                [q_ref[hh * HEAD_DIM:(hh + 1) * HEAD_DIM, :], qtail_ref[own, hh]], axis=0)
            m_refs[hh][slot] = jnp.full((1, blk), NEG_INF, F32)
            acc_refs[hh][slot] = jnp.zeros((HEAD_DIM + SUM_ROWS, blk), F32)

    def unit(t):
        n = t - 2
        slot = jnp.where(t < 2, t, jnp.where(n >= blk_b, 1, 0))
        j = jnp.where(t == 0, blk_b, jnp.where(t == 1, blk_a, jnp.where(n >= blk_b, n - blk_b, n)))
        return slot, j

    def score(hh, slot, j, buf, own):
        start = pl.multiple_of(j * blk, blk)
        q_aug = jnp.concatenate([qaug_ref[slot, hh], zeros_tail], axis=0)
        s = jnp.dot(k_ref[hh, pl.ds(start, blk), :], q_aug, preferred_element_type=F32)
        if own:
            s = jnp.where(causal, s, NEG_INF)
        s_ref[buf, hh] = s
        cmax_ref[buf, hh] = jnp.max(s, axis=0, keepdims=True)

    def accumulate(hh, slot, j, buf):
        m_old = m_refs[hh][slot]
        m_new = jnp.maximum(m_old, cmax_ref[buf, hh])
        e = jnp.exp2(s_ref[buf, hh] - m_new).astype(BF16)
        vt = jnp.concatenate([vt_ref[j, hh * HEAD_DIM:(hh + 1) * HEAD_DIM, :], ones_rows], axis=0)
        pv = jnp.dot(vt, e, preferred_element_type=F32)
        m_refs[hh][slot] = m_new
        acc_refs[hh][slot] = jnp.exp2(m_old - m_new) * acc_refs[hh][slot] + pv

    n_units = n_blk + 1
    for t in range(2):
        for hh in heads:
            score(hh, *unit(t), t, True)

    next_a, next_b = jnp.minimum(blk_a + 1, n_blk // 2 - 1), jnp.maximum(blk_b - 1, n_blk // 2)

    def stages(t, buf):
        now, ahead = unit(t), unit(t + 2)
        for hh in heads:
            accumulate(hh, *now, buf)
            score(hh, *ahead, buf, False)

    n_loop = (n_units - 2) // ATTN_UNROLL * ATTN_UNROLL

    def body(it, carry):
        gate_blocks([jnp.where(it == 0, next_a, next_b)] if n_iter > 1 else [next_a, next_b])
        for k in range(ATTN_UNROLL):
            stages(it * ATTN_UNROLL + k, k % 2)
        return carry

    n_iter = n_loop // ATTN_UNROLL
    lax.fori_loop(0, n_iter, body, 0)
    for t in range(n_loop, n_units - 2):
        stages(t, t % 2)
    for t in range(n_units - 2, n_units):
        for hh in heads:
            accumulate(hh, *unit(t), t % 2)

    for slot in range(2):
        outs = []
        for hh in heads:
            acc = acc_refs[hh][slot]
            outs.append(acc[:HEAD_DIM] * (1.0 / acc[HEAD_DIM:HEAD_DIM + 1]))
        o_ref[1 - slot] = jnp.concatenate(outs, axis=0).T.astype(BF16)


def _moba(qt, k, vt, km, *, batch, seq, aw):
    n_blk = seq // MOBA_BLOCK
    n_h = ATTN_HEADS_PER_STEP
    width = n_h * HEAD_DIM
    kern = functools.partial(_moba_kernel, n_blk=n_blk)
    return pl.pallas_call(
        kern,
        grid=(batch, aw // width, n_blk // 2),
        in_specs=[pl.BlockSpec((None, None, width, MOBA_BLOCK), lambda b, hg, p: (b, p, hg, 0)),
                  pl.BlockSpec((None, None, width, MOBA_BLOCK), lambda b, hg, p: (b, n_blk - 1 - p, hg, 0)),
                  pl.BlockSpec((None, n_blk, width, MOBA_BLOCK), lambda b, hg, p: (b, 0, hg, 0)),
                  pl.BlockSpec((None, n_blk, width), lambda b, hg, p: (b, 0, hg)),
                  pl.BlockSpec((None, n_h, seq, LANES), lambda b, hg, p: (b, hg, 0, 0)),
                  pl.BlockSpec((None, n_blk, width, MOBA_BLOCK), lambda b, hg, p: (b, 0, hg, 0))],
        out_specs=pl.BlockSpec((None, 2, None, MOBA_BLOCK, width), lambda b, hg, p: (b, 0, p, 0, hg)),
        out_shape=jax.ShapeDtypeStruct((batch, 2, n_blk // 2, MOBA_BLOCK, aw), BF16),
        scratch_shapes=[pltpu.VMEM((n_blk, n_h, n_blk, MOBA_BLOCK), BF16),
                        pltpu.VMEM((2, n_h, HEAD_DIM + n_blk, MOBA_BLOCK), BF16),
                        pltpu.VMEM((2, n_h, MOBA_BLOCK, MOBA_BLOCK), F32),
                        pltpu.VMEM((2, n_h, 1, MOBA_BLOCK), F32)]
        + [pltpu.VMEM((2, 1, MOBA_BLOCK), F32)] * n_h
        + [pltpu.VMEM((2, HEAD_DIM + SUM_ROWS, MOBA_BLOCK), F32)] * n_h,
        compiler_params=_cparams(3),
        name="moba",
    )(qt, qt, qt, km, k, vt)


def _mixffn_kernel(h_ref, mod_ref, *refs, n_attn, aw, gw, chunk, d_ff, tf, final_norm):
    attn_refs = refs[:n_attn]
    u_ref, vln_ref, ws_ref, bias_ref, ga_ref, gg_ref, wout_ref, nw_ref, wgu_ref, wd_ref, nf_ref, o_ref = refs[n_attn:]
    tm = h_ref.shape[0]
    gt = mod_ref[0, 5:6, :]
    attn = jnp.concatenate([a[...] for a in attn_refs], axis=0).astype(F32)
    attn_n = _rmsnorm(attn, ga_ref[...]).astype(BF16)

    n_grp = gw // LANES
    row = lax.broadcasted_iota(jnp.int32, (HEADS_PER_SLAB * chunk, chunk), 0)
    col = lax.broadcasted_iota(jnp.int32, (HEADS_PER_SLAB * chunk, chunk), 1)
    keep = (row % chunk) >= col
    w_pairs = [jnp.where(keep, ws_ref[g], 0.0).astype(BF16) for g in range(n_grp)]
    lane = lax.broadcasted_iota(jnp.int32, (chunk, LANES), 1)
    first_head = lane < HEAD_DIM
    bias = bias_ref[...]

    rows = []
    for c in range(tm // chunk):
        vc = vln_ref[c * chunk:(c + 1) * chunk, :]
        mixed = []
        for g in range(n_grp):
            r = jnp.dot(w_pairs[g], vc[:, g * LANES:(g + 1) * LANES], preferred_element_type=F32)
            mixed.append(jnp.where(first_head, r[:chunk], r[chunk:]))
        mixed = jnp.concatenate(mixed, axis=1) + bias
        rows.append(u_ref[c * chunk:(c + 1) * chunk, :].astype(F32) * mixed)
    gm = jnp.concatenate(rows, axis=0)
    gmlp_n = _rmsnorm(gm, gg_ref[...]).astype(BF16)

    y = jnp.dot(attn_n, wout_ref[:aw, :], preferred_element_type=F32)
    y = y + jnp.dot(gmlp_n, wout_ref[aw:, :], preferred_element_type=F32)
    h2 = h_ref[...] + gt * y
    out = _swiglu_halfstep(h2, mod_ref, nw_ref, wgu_ref, wd_ref, mod_base=6, d_ff=d_ff, tf=tf)
    if final_norm:
        out = _rmsnorm(out, nf_ref[...])
    o_ref[...] = out


def _mixffn(h, mod, attn, u, vln, w_s, b_s, g_attn, g_gmlp, w_out, norm_w, w_gu, w_down, norm_final,
            *, seq, tm, final_norm):
    n_tok, D = h.shape
    aw, gw = g_attn.shape[0], g_gmlp.shape[0]
    n_gh, chunk = w_s.shape[0], w_s.shape[1]
    d_ff = w_down.shape[0]
    st = seq // tm
    n_attn = tm // MOBA_BLOCK
    half = seq // MOBA_BLOCK // 2
    ws_pairs = w_s.reshape(n_gh // HEADS_PER_SLAB, HEADS_PER_SLAB * chunk, chunk)
    bias = jnp.repeat(b_s.T, gw // n_gh, axis=1)

    def attn_spec(which):
        def index(i):
            j = (i % st) * n_attn + which
            return (i // st, j // half, jnp.where(j < half, j, 2 * half - 1 - j), 0, 0)
        return pl.BlockSpec((None, None, None, MOBA_BLOCK, aw), index)

    kern = functools.partial(_mixffn_kernel, n_attn=n_attn, aw=aw, gw=gw, chunk=chunk, d_ff=d_ff,
                             tf=_ffn_tile(d_ff), final_norm=final_norm)
    return pl.pallas_call(
        kern,
        grid=(n_tok // tm,),
        in_specs=[pl.BlockSpec((tm, D), lambda i: (i, 0)),
                  pl.BlockSpec((1, N_MOD, D), lambda i: (i // st, 0, 0))]
        + [attn_spec(w) for w in range(n_attn)]
        + [pl.BlockSpec((tm, gw), lambda i: (i, 0)),
                  pl.BlockSpec((tm, gw), lambda i: (i, 0)),
                  _resident(ws_pairs.shape),
                  _resident((chunk, gw)),
                  _resident((1, aw)),
                  _resident((1, gw)),
                  _resident((aw + gw, D)),
                  _resident((1, D)),
                  _resident((D, 2 * d_ff)),
                  _resident((d_ff, D)),
                  _resident((1, D))],
        out_specs=pl.BlockSpec((tm, D), lambda i: (i, 0)),
        out_shape=jax.ShapeDtypeStruct((n_tok, D), F32),
        compiler_params=_cparams(1),
        name="mixffn",
    )(h, mod, *[attn] * n_attn, u, vln, ws_pairs, bias, g_attn.reshape(1, aw), g_gmlp.reshape(1, gw),
      w_out, norm_w.reshape(1, D), w_gu, w_down, norm_final.reshape(1, D))


def kernel(x, c, w_ada, b_ada, norm_ffn1, w_ffn1_gu, w_ffn1_down, norm_mix, w_in, gmlp_ln_g, gmlp_ln_b,
           gmlp_w_s, gmlp_b_s, g_attn_out, g_gmlp_out, w_out, norm_ffn2, w_ffn2_gu, w_ffn2_down, norm_final):
    B, S, D = x.shape
    depth = w_ada.shape[0]
    aw, gw = g_attn_out.shape[1], g_gmlp_out.shape[1]
    n_blk = S // MOBA_BLOCK
    assert S % (2 * MOBA_BLOCK) == 0 and aw % (ATTN_HEADS_PER_STEP * HEAD_DIM) == 0 and gw % LANES == 0
    assert n_blk <= LANES - HEAD_DIM and n_blk % 16 == 0
    tm = 2 * MOBA_BLOCK
    tm_ffn = 1024 if S % 1024 == 0 else tm
    h = x.reshape(B * S, D)
    for l in range(depth):
        last = l == depth - 1
        mod, (w_gu1, w_down1) = _ada(c, w_ada[l], b_ada[l], [w_ffn1_gu, w_ffn1_down], l)
        h, (w_in_l, w_out_l, w_gu2, w_down2) = _ffn(
            h, mod, norm_ffn1[l], w_gu1, w_down1,
            [w_in, w_out, w_ffn2_gu, w_ffn2_down], l, seq=S, mod_base=0, tm=tm_ffn)
        qt, k, vt, km, u, vln = _inproj(h, mod, norm_mix[l], w_in_l, gmlp_ln_g[l], gmlp_ln_b[l],
                                        batch=B, seq=S, aw=aw, gw=gw, tm=tm)
        attn = _moba(qt, k, vt, km, batch=B, seq=S, aw=aw)
        h = _mixffn(h, mod, attn, u, vln, gmlp_w_s[l], gmlp_b_s[l], g_attn_out[l], g_gmlp_out[l], w_out_l,
                    norm_ffn2[l], w_gu2, w_down2, norm_final, seq=S, tm=tm, final_norm=last)
    return h.reshape(B, S, D)
```
